```python
import math
import jax
import jax.numpy as jnp
from jax import lax
import numpy as np

D_MODEL = 4096
BATCH = 4
SEQ = 2048
DEPTH = 2
DEC_BATCH = 8
DEC_SEQ = 8
PAST_LEN = 16384
PAGE_SIZE = 128

N_A_LAYERS = DEPTH // 2
N_B_LAYERS = DEPTH - N_A_LAYERS

DN_QK_HEADS = 16
DN_V_HEADS = 32
DN_HEAD_DIM = 128
DN_KEY_DIM = DN_QK_HEADS * DN_HEAD_DIM
DN_VAL_DIM = DN_V_HEADS * DN_HEAD_DIM
DN_CONV_W = 4
DN_CONV_DIM = 2 * DN_KEY_DIM + DN_VAL_DIM
DN_IN_DIM = DN_CONV_DIM + DN_VAL_DIM + 2 * DN_V_HEADS
DN_CHUNK = 64

GROUP_WINDOWS = (128, 512, 2048)
GROUP_DILATIONS = (1, 4, 16)
N_GROUPS = 3
HEADS_PER_GROUP = 16
ATT_HEAD_DIM = 128
N_ATT_HEADS = N_GROUPS * HEADS_PER_GROUP
ATT_SLOT_DIM = HEADS_PER_GROUP * ATT_HEAD_DIM
ATT_Q_DIM = N_ATT_HEADS * ATT_HEAD_DIM
ATT_KV_DIM = 2 * N_ATT_HEADS * ATT_HEAD_DIM
ALIBI_MAX_BIAS = 8.0

N_EXPERTS = 32
TOP_K = 4
D_EXPERT = D_MODEL // 2
SWIGLU_LIMIT = 7.0
SWIGLU_ALPHA = 1.702

LN_EPS = 1e-5
NORM_EPS = 1e-6
DEEP_ALPHA = (2.0 * DEPTH) ** 0.25
DEEP_BETA = (8.0 * DEPTH) ** -0.25

kernel_name = 'yoco_deltanet_dilated_moe_step'


def layer_norm(x, g, b):
    xf = x.astype(jnp.float32)
    mu = jnp.mean(xf, axis=-1, keepdims=True)
    var = jnp.mean(jnp.square(xf - mu), axis=-1, keepdims=True)
    y = (xf - mu) * lax.rsqrt(var + LN_EPS) * g.astype(jnp.float32) + b.astype(jnp.float32)
    return y.astype(x.dtype)


def post_norm(x, f, g, b):
    return layer_norm(DEEP_ALPHA * x + f.astype(x.dtype), g, b)


def l2_normalize(t):
    tf = t.astype(jnp.float32)
    return tf * lax.rsqrt(jnp.sum(tf * tf, axis=-1, keepdims=True) + NORM_EPS)


def causal_short_conv(u, buf, w):
    t_new = u.shape[1]
    ext = jnp.concatenate([buf.astype(u.dtype), u], axis=1)
    y = ext[:, 0:t_new] * w[0]
    for j in range(1, DN_CONV_W):
        y = y + ext[:, j:j + t_new] * w[j]
    return jax.nn.silu(y), ext[:, t_new:]


def gated_delta_chunked(q, k, v, log_decay, beta, s0):
    b, t, h, dk = q.shape
    dv = v.shape[-1]
    c = DN_CHUNK
    n = -(-t // c)
    pad = n * c - t

    def prep(a):
        a = jnp.pad(a, [(0, 0), (0, pad)] + [(0, 0)] * (a.ndim - 2))
        a = a.reshape((b, n, c) + a.shape[2:])
        return a.transpose((1, 0, 3, 2) + tuple(range(4, a.ndim)))

    qc, kc, vc, bc = prep(q), prep(k), prep(v), prep(beta)
    gc = jnp.cumsum(prep(log_decay), axis=-1)
    idx = jnp.arange(c)
    strict = idx[:, None] > idx[None, :]
    incl = idx[:, None] >= idx[None, :]
    diff = gc[..., :, None] - gc[..., None, :]
    decay_strict = jnp.exp(jnp.where(strict, diff, -jnp.inf))
    decay_incl = jnp.exp(jnp.where(incl, diff, -jnp.inf))
    kb = kc * bc[..., None]
    lower = jnp.einsum('nbhid,nbhjd->nbhij', kb, kc) * decay_strict
    a_mat = lower + jnp.eye(c, dtype=jnp.float32)
    value = lax.linalg.triangular_solve(a_mat, vc * bc[..., None], left_side=True, lower=True)
    kcum = lax.linalg.triangular_solve(a_mat, kb * jnp.exp(gc)[..., None], left_side=True, lower=True)
    attn = jnp.einsum('nbhid,nbhjd->nbhij', qc, kc) * decay_incl
    q_g = qc * jnp.exp(gc)[..., None]
    g_last = gc[..., -1]
    k_tail = kc * jnp.exp(g_last[..., None] - gc)[..., None]

    def step(s, xs):
        value_c, kcum_c, attn_c, qg_c, kt_c, gl_c = xs
        v_new = value_c - jnp.einsum('bhcd,bhde->bhce', kcum_c, s)
        o_c = jnp.einsum('bhcd,bhde->bhce', qg_c, s) + jnp.einsum('bhij,bhje->bhie', attn_c, v_new)
        s = s * jnp.exp(gl_c)[..., None, None] + jnp.einsum('bhcd,bhce->bhde', kt_c, v_new)
        return s, o_c

    s_fin, o = lax.scan(step, s0.astype(jnp.float32), (value, kcum, attn, q_g, k_tail, g_last))
    o = o.transpose(1, 0, 3, 2, 4).reshape(b, n * c, h, dv)[:, :t]
    return o, s_fin


def deltanet_mixer(x, conv_buf, s0, w_in, w_conv, a_log, dt_bias, norm_w, w_out):
    b, t, _ = x.shape
    proj = x @ w_in
    qkv, z, beta_logit, a_in = jnp.split(
        proj, [DN_CONV_DIM, DN_CONV_DIM + DN_VAL_DIM, DN_CONV_DIM + DN_VAL_DIM + DN_V_HEADS], axis=-1)
    qkv, new_buf = causal_short_conv(qkv, conv_buf, w_conv)
    q, k, v = jnp.split(qkv, [DN_KEY_DIM, 2 * DN_KEY_DIM], axis=-1)
    rep = DN_V_HEADS // DN_QK_HEADS
    q = jnp.repeat(l2_normalize(q.reshape(b, t, DN_QK_HEADS, DN_HEAD_DIM)), rep, axis=2) * DN_HEAD_DIM ** -0.5
    k = jnp.repeat(l2_normalize(k.reshape(b, t, DN_QK_HEADS, DN_HEAD_DIM)), rep, axis=2)
    v = v.reshape(b, t, DN_V_HEADS, DN_HEAD_DIM).astype(jnp.float32)
    beta = jax.nn.sigmoid(beta_logit.astype(jnp.float32))
    log_decay = -jnp.exp(a_log.astype(jnp.float32)) * jax.nn.softplus(
        a_in.astype(jnp.float32) + dt_bias.astype(jnp.float32))
    o, s_new = gated_delta_chunked(q, k, v, log_decay, beta, s0)
    o = o * lax.rsqrt(jnp.mean(o * o, axis=-1, keepdims=True) + NORM_EPS) * norm_w.astype(jnp.float32)
    o = o * jax.nn.silu(z.reshape(b, t, DN_V_HEADS, DN_HEAD_DIM).astype(jnp.float32))
    y = o.reshape(b, t, DN_VAL_DIM).astype(x.dtype) @ w_out
    return y, new_buf, s_new.astype(s0.dtype)


def alibi_slopes():
    h = jnp.arange(1, N_ATT_HEADS + 1, dtype=jnp.float32)
    return (2.0 ** (-ALIBI_MAX_BIAS * h / N_ATT_HEADS)).reshape(N_GROUPS, HEADS_PER_GROUP)


def split_kv_groups(kv):
    b, t, _ = kv.shape
    kv = kv.reshape(b, t, 2, N_GROUPS, HEADS_PER_GROUP, ATT_HEAD_DIM)
    return [kv[:, :, :, g] for g in range(N_GROUPS)]


def dilated_attention_prompt(q, k, v, window, dilation, slopes):
    b, s, h, hd = q.shape
    d = dilation
    n = s // d
    w = window // d
    blk = w
    nb = -(-n // blk)
    npad = nb * blk

    def stride_gather(t):
        return t.reshape(b, n, d, h, hd).transpose(0, 2, 1, 3, 4).reshape(b * d, n, h, hd)

    qs = jnp.pad(stride_gather(q), ((0, 0), (0, npad - n), (0, 0), (0, 0))).reshape(b * d, nb, blk, h, hd)

    def band(t):
        t = jnp.pad(stride_gather(t), ((0, 0), (blk, npad - n), (0, 0), (0, 0))).reshape(b * d, nb + 1, blk, h, hd)
        return jnp.concatenate([t[:, :-1], t[:, 1:]], axis=2)

    k_band, v_band = band(k), band(v)
    sc = jnp.einsum('znqhd,znkhd->znhqk', qs, k_band, preferred_element_type=jnp.float32) * hd ** -0.5
    qpos = jnp.arange(nb)[:, None] * blk + jnp.arange(blk)[None, :]
    kpos = jnp.arange(nb)[:, None] * blk - blk + jnp.arange(2 * blk)[None, :]
    dist = qpos[:, :, None] - kpos[:, None, :]
    valid = (dist >= 0) & (dist <= w) & (kpos[:, None, :] >= 0)
    bias = -slopes[None, :, None, None] * (dist * d).astype(jnp.float32)[:, None]
    sc = jnp.where(valid[:, None], sc + bias, -jnp.inf)
    lse = jax.nn.logsumexp(sc, axis=-1)
    p = jnp.exp(sc - lse[..., None])
    o = jnp.einsum('znhqk,znkhd->znqhd', p, v_band.astype(jnp.float32)).reshape(b * d, npad, h, hd)[:, :n]
    o = o.reshape(b, d, n, h, hd).transpose(0, 2, 1, 3, 4).reshape(b, s, h, hd)
    lse = lse.transpose(0, 1, 3, 2).reshape(b * d, npad, h)[:, :n]
    lse = lse.reshape(b, d, n, h).transpose(0, 2, 1, 3).reshape(b, s, h)
    return o, lse


def dilated_attention_sample(q, kv_ext, window, dilation, slopes):
    t = q.shape[1]
    l = kv_ext.shape[1] - t
    hd = q.shape[-1]
    j = jnp.arange(window // dilation + 1)
    idx = (l + jnp.arange(t))[:, None] - dilation * j[None, :]
    valid = idx >= 0
    kv_g = jnp.take(kv_ext, jnp.maximum(idx, 0), axis=1)
    sc = jnp.einsum('bthd,btjhd->bhtj', q, kv_g[:, :, :, 0], preferred_element_type=jnp.float32) * hd ** -0.5
    sc = sc - slopes[None, :, None, None] * (dilation * j).astype(jnp.float32)[None, None, None, :]
    sc = jnp.where(valid[None, None], sc, -jnp.inf)
    lse = jax.nn.logsumexp(sc, axis=-1)
    p = jnp.exp(sc - lse[..., None])
    o = jnp.einsum('bhtj,btjhd->bthd', p, kv_g[:, :, :, 1].astype(jnp.float32))
    return o, lse.transpose(0, 2, 1)


def merge_groups(outs, lses):
    wts = jax.nn.softmax(jnp.stack(lses, axis=0), axis=0)
    return jnp.einsum('gbth,gbthd->bthd', wts, jnp.stack(outs, axis=0))


def dilated_mixer_prompt(x, kv_groups, w_q, w_out, slopes):
    b, t, _ = x.shape
    q = (x @ w_q).reshape(b, t, N_GROUPS, HEADS_PER_GROUP, ATT_HEAD_DIM)
    outs, lses = [], []
    for g in range(N_GROUPS):
        o, lse = dilated_attention_prompt(q[:, :, g], kv_groups[g][:, :, 0], kv_groups[g][:, :, 1],
                                          GROUP_WINDOWS[g], GROUP_DILATIONS[g], slopes[g])
        outs.append(o)
        lses.append(lse)
    o = merge_groups(outs, lses)
    return o.reshape(b, t, ATT_SLOT_DIM).astype(x.dtype) @ w_out


def dilated_mixer_sample(x, kv_ext_groups, w_q, w_out, slopes):
    b, t, _ = x.shape
    q = (x @ w_q).reshape(b, t, N_GROUPS, HEADS_PER_GROUP, ATT_HEAD_DIM)
    outs, lses = [], []
    for g in range(N_GROUPS):
        o, lse = dilated_attention_sample(q[:, :, g], kv_ext_groups[g], GROUP_WINDOWS[g], GROUP_DILATIONS[g], slopes[g])
        outs.append(o)
        lses.append(lse)
    o = merge_groups(outs, lses)
    return o.reshape(b, t, ATT_SLOT_DIM).astype(x.dtype) @ w_out


def clamped_swiglu(h):
    x_glu = jnp.minimum(h[..., ::2], SWIGLU_LIMIT)
    x_lin = jnp.clip(h[..., 1::2], -SWIGLU_LIMIT, SWIGLU_LIMIT)
    return x_glu * jax.nn.sigmoid(SWIGLU_ALPHA * x_glu) * (x_lin + 1.0)


def moe_ffn(x, w_router, b_router, w_up, b_up, w_down, b_down):
    logits = jnp.matmul(x, w_router, preferred_element_type=jnp.float32) + b_router.astype(jnp.float32)
    top_vals, top_idx = lax.top_k(logits, TOP_K)
    gates = jax.nn.softmax(top_vals, axis=-1)
    dense_gate = jnp.einsum('nk,nke->ne', gates, jax.nn.one_hot(top_idx, N_EXPERTS, dtype=jnp.float32))
    y = jnp.zeros(x.shape, jnp.float32)
    for e in range(N_EXPERTS):
        h = clamped_swiglu(x @ w_up[e] + b_up[e])
        y = y + dense_gate[:, e:e + 1] * (h @ w_down[e] + b_down[e]).astype(jnp.float32)
    return y.astype(x.dtype)


def setup_inputs(seed: int = 0) -> dict:
    key = jax.random.key(seed)
    ks = jax.random.split(key, 24)
    f32 = jnp.float32

    def nrm(k, shape, scale):
        return jax.random.normal(k, shape, f32) * scale

    dt = jnp.exp(jax.random.uniform(ks[9], (N_A_LAYERS, DN_V_HEADS), f32, math.log(1e-3), math.log(1e-1)))
    return {
        'x_prompt': nrm(ks[0], (BATCH, SEQ, D_MODEL), 1.0),
        'x_sample': nrm(ks[1], (DEC_BATCH, DEC_SEQ, D_MODEL), 1.0),
        'state_dn_S': nrm(ks[2], (N_A_LAYERS, DEC_BATCH, DN_V_HEADS, DN_HEAD_DIM, DN_HEAD_DIM), DN_HEAD_DIM ** -0.5),
        'state_dn_conv': nrm(ks[3], (N_A_LAYERS, DEC_BATCH, DN_CONV_W - 1, DN_CONV_DIM), 1.0),
        'cache_kv_w128': nrm(ks[4], (DEC_BATCH, min(GROUP_WINDOWS[0], PAST_LEN), 2, HEADS_PER_GROUP, ATT_HEAD_DIM), 1.0),
        'cache_kv_w512': nrm(ks[5], (DEC_BATCH, min(GROUP_WINDOWS[1], PAST_LEN), 2, HEADS_PER_GROUP, ATT_HEAD_DIM), 1.0),
        'cache_kv_w2048': nrm(ks[6], (DEC_BATCH, min(GROUP_WINDOWS[2], PAST_LEN), 2, HEADS_PER_GROUP, ATT_HEAD_DIM), 1.0),
        'w_dn_in': nrm(ks[7], (N_A_LAYERS, D_MODEL, DN_IN_DIM), D_MODEL ** -0.5),
        'w_dn_conv': nrm(ks[8], (N_A_LAYERS, DN_CONV_W, DN_CONV_DIM), DN_CONV_W ** -0.5),
        'dn_a_log': jnp.log(jax.random.uniform(ks[10], (N_A_LAYERS, DN_V_HEADS), f32, 1.0, 16.0)),
        'dn_dt_bias': dt + jnp.log(-jnp.expm1(-dt)),
        'dn_norm_w': 1.0 + nrm(ks[11], (N_A_LAYERS, DN_HEAD_DIM), 0.02),
        'w_dn_out': nrm(ks[12], (N_A_LAYERS, DN_VAL_DIM, D_MODEL), DN_VAL_DIM ** -0.5 * DEEP_BETA),
        'w_kv_shared': nrm(ks[13], (D_MODEL, ATT_KV_DIM), D_MODEL ** -0.5),
        'w_att_q': nrm(ks[14], (N_B_LAYERS, D_MODEL, ATT_Q_DIM), D_MODEL ** -0.5),
        'w_att_out': nrm(ks[15], (N_B_LAYERS, ATT_SLOT_DIM, D_MODEL), ATT_SLOT_DIM ** -0.5 * DEEP_BETA),
        'ln_g': 1.0 + nrm(ks[16], (DEPTH, 2, D_MODEL), 0.02),
        'ln_b': nrm(ks[17], (DEPTH, 2, D_MODEL), 0.02),
        'w_router': nrm(ks[18], (DEPTH, D_MODEL, N_EXPERTS), D_MODEL ** -0.5),
        'b_router': nrm(ks[19], (DEPTH, N_EXPERTS), 0.01),
        'w_up': nrm(ks[20], (DEPTH, N_EXPERTS, D_MODEL, 2 * D_EXPERT), D_MODEL ** -0.5),
        'b_up': nrm(ks[21], (DEPTH, N_EXPERTS, 2 * D_EXPERT), 0.01),
        'w_down': nrm(ks[22], (DEPTH, N_EXPERTS, D_EXPERT, D_MODEL), D_EXPERT ** -0.5 * DEEP_BETA),
        'b_down': nrm(ks[23], (DEPTH, N_EXPERTS, D_MODEL), 0.01),
    }


def reference(x_prompt, x_sample, state_dn_S, state_dn_conv, cache_kv_w128, cache_kv_w512, cache_kv_w2048,
              w_dn_in, w_dn_conv, dn_a_log, dn_dt_bias, dn_norm_w, w_dn_out, w_kv_shared, w_att_q, w_att_out,
              ln_g, ln_b, w_router, b_router, w_up, b_up, w_down, b_down):
    bp, sp, _ = x_prompt.shape
    n_prompt_tokens = bp * sp
    t_sample = x_sample.shape[1]
    slopes = alibi_slopes()
    kv_caches = (cache_kv_w128, cache_kv_w512, cache_kv_w2048)
    xp, xs = x_prompt, x_sample
    dn_s_p, dn_conv_p, dn_s_s, dn_conv_s = [], [], [], []
    kv_p, kv_ext_s, new_kv_p, new_kv_s = [], [], [], []
    for layer in range(DEPTH):
        if layer < N_A_LAYERS:
            i = layer
            conv_zero = jnp.zeros((bp, DN_CONV_W - 1, DN_CONV_DIM), xp.dtype)
            s_zero = jnp.zeros((bp, DN_V_HEADS, DN_HEAD_DIM, DN_HEAD_DIM), jnp.float32)
            mix_p, cbuf_p, s_p = deltanet_mixer(xp, conv_zero, s_zero, w_dn_in[i], w_dn_conv[i], dn_a_log[i],
                                                dn_dt_bias[i], dn_norm_w[i], w_dn_out[i])
            mix_s, cbuf_s, s_s = deltanet_mixer(xs, state_dn_conv[i], state_dn_S[i], w_dn_in[i], w_dn_conv[i],
                                                dn_a_log[i], dn_dt_bias[i], dn_norm_w[i], w_dn_out[i])
            dn_s_p.append(s_p)
            dn_conv_p.append(cbuf_p)
            dn_s_s.append(s_s)
            dn_conv_s.append(cbuf_s)
        else:
            i = layer - N_A_LAYERS
            mix_p = dilated_mixer_prompt(xp, kv_p, w_att_q[i], w_att_out[i], slopes)
            mix_s = dilated_mixer_sample(xs, kv_ext_s, w_att_q[i], w_att_out[i], slopes)
        xp = post_norm(xp, mix_p, ln_g[layer, 0], ln_b[layer, 0])
        xs = post_norm(xs, mix_s, ln_g[layer, 0], ln_b[layer, 0])
        tokens = jnp.concatenate([xp.reshape(-1, D_MODEL), xs.reshape(-1, D_MODEL)], axis=0)
        ffn = moe_ffn(tokens, w_router[layer], b_router[layer], w_up[layer], b_up[layer], w_down[layer], b_down[layer])
        tokens = post_norm(tokens, ffn, ln_g[layer, 1], ln_b[layer, 1])
        xp = tokens[:n_prompt_tokens].reshape(xp.shape)
        xs = tokens[n_prompt_tokens:].reshape(xs.shape)
        if layer == N_A_LAYERS - 1:
            kv_p = split_kv_groups(xp @ w_kv_shared)
            kv_s_new = split_kv_groups(xs @ w_kv_shared)
            kv_ext_s = [jnp.concatenate([kv_caches[g], kv_s_new[g].astype(kv_caches[g].dtype)], axis=1)
                        for g in range(N_GROUPS)]
            new_kv_p = [kv_p[g][:, -min(GROUP_WINDOWS[g], sp):] for g in range(N_GROUPS)]
            new_kv_s = [kv_ext_s[g][:, t_sample:] for g in range(N_GROUPS)]
    y_prompt, y_sample = xp, xs
    state_dn_S_prompt = jnp.stack(dn_s_p, axis=0)
    state_dn_conv_prompt = jnp.stack(dn_conv_p, axis=0)
    state_dn_S_sample = jnp.stack(dn_s_s, axis=0)
    state_dn_conv_sample = jnp.stack(dn_conv_s, axis=0)
    cache_kv_w128_prompt, cache_kv_w512_prompt, cache_kv_w2048_prompt = new_kv_p
    cache_kv_w128_sample, cache_kv_w512_sample, cache_kv_w2048_sample = new_kv_s
    return (y_prompt, y_sample, state_dn_S_prompt, state_dn_conv_prompt, cache_kv_w128_prompt,
            cache_kv_w512_prompt, cache_kv_w2048_prompt, state_dn_S_sample, state_dn_conv_sample,
            cache_kv_w128_sample, cache_kv_w512_sample, cache_kv_w2048_sample)
```

```python
import functools
import math

import jax
import jax.numpy as jnp
from jax import lax
from jax.experimental import pallas as pl
from jax.experimental.pallas import tpu as pltpu

D_MODEL = 4096
DEPTH = 2
N_A_LAYERS = DEPTH // 2

DN_QK_HEADS = 16
DN_V_HEADS = 32
DN_HEAD_DIM = 128
DN_KEY_DIM = DN_QK_HEADS * DN_HEAD_DIM
DN_VAL_DIM = DN_V_HEADS * DN_HEAD_DIM
DN_CONV_W = 4
DN_CONV_DIM = 2 * DN_KEY_DIM + DN_VAL_DIM
DN_IN_DIM = DN_CONV_DIM + DN_VAL_DIM + 2 * DN_V_HEADS
DN_CHUNK = 64

GROUP_WINDOWS = (128, 512, 2048)
GROUP_DILATIONS = (1, 4, 16)
N_GROUPS = 3
HEADS_PER_GROUP = 16
ATT_HEAD_DIM = 128
N_ATT_HEADS = N_GROUPS * HEADS_PER_GROUP
ATT_SLOT_DIM = HEADS_PER_GROUP * ATT_HEAD_DIM
ALIBI_MAX_BIAS = 8.0

N_EXPERTS = 32
TOP_K = 4
D_EXPERT = D_MODEL // 2
SWIGLU_LIMIT = 7.0
SWIGLU_ALPHA = 1.702

LN_EPS = 1e-5
NORM_EPS = 1e-6
DEEP_ALPHA = (2.0 * DEPTH) ** 0.25

LANES = 128
MXU_DIM = 256
VMEM_LIMIT_BYTES = 56 * 1024 * 1024

DENSE_TM = 688
DENSE_TN = 512
MOE_TM = 256
MOE_UP_TN = 512
MOE_DOWN_TN = 512

BF16 = jnp.bfloat16
F32 = jnp.float32


def _dense_kernel(x_ref, w_ref, o_ref, wbf_ref):
    @pl.when(pl.program_id(1) == 0)
    def _():
        wbf_ref[...] = w_ref[...].astype(BF16)

    o_ref[...] = jnp.dot(x_ref[...], wbf_ref[...], preferred_element_type=F32).astype(o_ref.dtype)


def _dense(x, w, *, col_block_offset=0, n_cols=None, tn=DENSE_TN, out_dtype=F32):
    m, k = x.shape
    n_cols = w.shape[1] if n_cols is None else n_cols
    tm = DENSE_TM if m % DENSE_TM == 0 else m
    assert m % tm == 0 and n_cols % tn == 0
    return pl.pallas_call(
        _dense_kernel,
        grid=(n_cols // tn, m // tm),
        in_specs=[pl.BlockSpec((tm, k), lambda j, i: (i, 0)),
                  pl.BlockSpec((k, tn), lambda j, i: (0, j + col_block_offset))],
        out_specs=pl.BlockSpec((tm, tn), lambda j, i: (i, j)),
        out_shape=jax.ShapeDtypeStruct((m, n_cols), out_dtype),
        scratch_shapes=[pltpu.VMEM((k, tn), BF16)],
        compiler_params=pltpu.CompilerParams(dimension_semantics=("arbitrary", "arbitrary"),
                                             vmem_limit_bytes=VMEM_LIMIT_BYTES),
        name="dense_proj",
    )(x, w)


def _moe_rows(n_tokens):
    n_assign = n_tokens * TOP_K
    n_tiles = -(-(n_assign + N_EXPERTS * (MOE_TM - 1)) // MOE_TM)
    return n_tiles, n_tiles * MOE_TM


def _route(logits):
    n_tokens = logits.shape[0]
    n_tiles, n_rows = _moe_rows(n_tokens)
    top_vals, top_idx = lax.top_k(logits, TOP_K)
    gates = jax.nn.softmax(top_vals, axis=-1).reshape(-1)
    flat_e = top_idx.reshape(-1).astype(jnp.int32)
    onehot = (flat_e[:, None] == jnp.arange(N_EXPERTS, dtype=jnp.int32)[None, :]).astype(jnp.int32)
    csum = jnp.cumsum(onehot, axis=0)
    rank = jnp.take_along_axis(csum, flat_e[:, None], axis=1)[:, 0] - 1
    counts = csum[-1]
    padded = ((counts + MOE_TM - 1) // MOE_TM) * MOE_TM
    pend = jnp.cumsum(padded)
    pstart = pend - padded
    row_of = pstart[flat_e] + rank
    src_tok = jnp.zeros((n_rows,), jnp.int32).at[row_of].set(jnp.arange(flat_e.shape[0], dtype=jnp.int32) // TOP_K)
    gate_row = jnp.zeros((n_rows,), F32).at[row_of].set(gates)
    n_used = (pend[-1] // MOE_TM).astype(jnp.int32)
    tile_start = jnp.minimum(jnp.arange(n_tiles, dtype=jnp.int32), n_used - 1) * MOE_TM
    tile_expert = jnp.minimum(jnp.searchsorted(pend, tile_start, side="right"), N_EXPERTS - 1).astype(jnp.int32)
    return row_of, src_tok, gate_row, tile_expert, n_used.reshape(1)


def _weights_changed(te_ref, i):
    return (i == 0) | (te_ref[i] != te_ref[jnp.maximum(i - 1, 0)])


def _moe_up_kernel(te_ref, nu_ref, x_ref, w_ref, b_ref, o_ref, wbf_ref):
    i = pl.program_id(1)

    @pl.when(_weights_changed(te_ref, i))
    def _():
        wbf_ref[...] = w_ref[0, 0].astype(BF16)

    @pl.when(i < nu_ref[0])
    def _():
        h = jnp.dot(x_ref[...], wbf_ref[...], preferred_element_type=F32) + b_ref[0]
        glu = jnp.minimum(h, SWIGLU_LIMIT)
        glu = glu * jax.nn.sigmoid(SWIGLU_ALPHA * glu)
        lin = jnp.clip(h, -SWIGLU_LIMIT, SWIGLU_LIMIT) + 1.0
        tn = h.shape[1]
        prod = glu * pltpu.roll(lin, tn - 1, 1)
        lane = lax.broadcasted_iota(jnp.int32, prod.shape, 1)
        prod = jnp.where(lane % 2 == 0, prod, 0.0).astype(BF16)
        r = lax.broadcasted_iota(jnp.int32, (MXU_DIM, MXU_DIM // 2), 0)
        c = lax.broadcasted_iota(jnp.int32, (MXU_DIM, MXU_DIM // 2), 1)
        sel = (r == 2 * c).astype(BF16)
        for q in range(tn // MXU_DIM):
            part = jnp.dot(prod[:, q * MXU_DIM:(q + 1) * MXU_DIM], sel, preferred_element_type=F32)
            o_ref[:, q * (MXU_DIM // 2):(q + 1) * (MXU_DIM // 2)] = part.astype(o_ref.dtype)


def _moe_down_kernel(te_ref, nu_ref, h_ref, w_ref, b_ref, g_ref, o_ref, wbf_ref):
    i = pl.program_id(1)

    @pl.when(_weights_changed(te_ref, i))
    def _():
        wbf_ref[...] = w_ref[0, 0].astype(BF16)

    @pl.when(i < nu_ref[0])
    def _():
        y = jnp.dot(h_ref[...], wbf_ref[...], preferred_element_type=F32) + b_ref[0]
        o_ref[...] = y * g_ref[...]


def _moe_experts(x_rows, gate_row, tile_expert, n_used, w_up, b_up, w_down, b_down, layer):
    n_rows = x_rows.shape[0]
    n_tiles = n_rows // MOE_TM
    tm = MOE_TM

    def row_tile(j, i, te, nu):
        return (jnp.minimum(i, nu[0] - 1), 0)

    tn = MOE_UP_TN
    hact = pl.pallas_call(
        _moe_up_kernel,
        grid_spec=pltpu.PrefetchScalarGridSpec(
            num_scalar_prefetch=2,
            grid=(2 * D_EXPERT // tn, n_tiles),
            in_specs=[pl.BlockSpec((tm, D_MODEL), row_tile),
                      pl.BlockSpec((1, 1, D_MODEL, tn), lambda j, i, te, nu: (layer, te[i], 0, j)),
                      pl.BlockSpec((1, 1, tn), lambda j, i, te, nu: (layer * N_EXPERTS + te[i], 0, j))],
            out_specs=pl.BlockSpec((tm, tn // 2), lambda j, i, te, nu: (jnp.minimum(i, nu[0] - 1), j)),
            scratch_shapes=[pltpu.VMEM((D_MODEL, tn), BF16)]),
        out_shape=jax.ShapeDtypeStruct((n_rows, D_EXPERT), BF16),
        compiler_params=pltpu.CompilerParams(dimension_semantics=("arbitrary", "arbitrary"),
                                             vmem_limit_bytes=VMEM_LIMIT_BYTES),
        name="moe_up_swiglu",
    )(tile_expert, n_used, x_rows, w_up, b_up.reshape(DEPTH * N_EXPERTS, 1, 2 * D_EXPERT))

    tn = MOE_DOWN_TN
    return pl.pallas_call(
        _moe_down_kernel,
        grid_spec=pltpu.PrefetchScalarGridSpec(
            num_scalar_prefetch=2,
            grid=(D_MODEL // tn, n_tiles),
            in_specs=[pl.BlockSpec((tm, D_EXPERT), row_tile),
                      pl.BlockSpec((1, 1, D_EXPERT, tn), lambda j, i, te, nu: (layer, te[i], 0, j)),
                      pl.BlockSpec((1, 1, tn), lambda j, i, te, nu: (layer * N_EXPERTS + te[i], 0, j)),
                      pl.BlockSpec((tm, 1), row_tile)],
            out_specs=pl.BlockSpec((tm, tn), lambda j, i, te, nu: (jnp.minimum(i, nu[0] - 1), j)),
            scratch_shapes=[pltpu.VMEM((D_EXPERT, tn), BF16)]),
        out_shape=jax.ShapeDtypeStruct((n_rows, D_MODEL), F32),
        compiler_params=pltpu.CompilerParams(dimension_semantics=("arbitrary", "arbitrary"),
                                             vmem_limit_bytes=VMEM_LIMIT_BYTES),
        name="moe_down",
    )(tile_expert, n_used, hact, w_down, b_down.reshape(DEPTH * N_EXPERTS, 1, D_MODEL), gate_row.reshape(n_rows, 1))


def _moe_ffn(x, w_router, b_router, w_up, b_up, w_down, b_down, layer):
    logits = jnp.matmul(x, w_router[layer], precision=lax.Precision.HIGHEST) + b_router[layer]
    row_of, src_tok, gate_row, tile_expert, n_used = _route(logits)
    x_rows = jnp.take(x.astype(BF16), src_tok, axis=0)
    y_rows = _moe_experts(x_rows, gate_row, tile_expert, n_used, w_up, b_up, w_down, b_down, layer)
    return jnp.take(y_rows, row_of, axis=0).reshape(x.shape[0], TOP_K, D_MODEL).sum(axis=1)


def _layer_norm(x, g, b):
    mu = jnp.mean(x, axis=-1, keepdims=True)
    var = jnp.mean(jnp.square(x - mu), axis=-1, keepdims=True)
    return (x - mu) * lax.rsqrt(var + LN_EPS) * g + b


def _post_norm(x, f, g, b):
    return _layer_norm(DEEP_ALPHA * x + f, g, b)


def _l2_normalize(t):
    return t * lax.rsqrt(jnp.sum(t * t, axis=-1, keepdims=True) + NORM_EPS)


def _causal_short_conv(u, buf, w):
    t_new = u.shape[1]
    ext = jnp.concatenate([buf, u], axis=1)
    y = ext[:, 0:t_new] * w[0]
    for j in range(1, DN_CONV_W):
        y = y + ext[:, j:j + t_new] * w[j]
    return jax.nn.silu(y), ext[:, t_new:]


def _gated_delta_chunked(q, k, v, log_decay, beta, s0):
    b, t, h, dk = q.shape
    dv = v.shape[-1]
    c = DN_CHUNK
    n = -(-t // c)
    pad = n * c - t

    def prep(a):
        a = jnp.pad(a, [(0, 0), (0, pad)] + [(0, 0)] * (a.ndim - 2))
        a = a.reshape((b, n, c) + a.shape[2:])
        return a.transpose((1, 0, 3, 2) + tuple(range(4, a.ndim)))

    qc, kc, vc, bc = prep(q), prep(k), prep(v), prep(beta)
    gc = jnp.cumsum(prep(log_decay), axis=-1)
    idx = jnp.arange(c)
    strict = idx[:, None] > idx[None, :]
    incl = idx[:, None] >= idx[None, :]
    diff = gc[..., :, None] - gc[..., None, :]
    decay_strict = jnp.exp(jnp.where(strict, diff, -jnp.inf))
    decay_incl = jnp.exp(jnp.where(incl, diff, -jnp.inf))
    kb = kc * bc[..., None]
    lower = jnp.einsum('nbhid,nbhjd->nbhij', kb, kc) * decay_strict
    a_mat = lower + jnp.eye(c, dtype=F32)
    value = lax.linalg.triangular_solve(a_mat, vc * bc[..., None], left_side=True, lower=True)
    kcum = lax.linalg.triangular_solve(a_mat, kb * jnp.exp(gc)[..., None], left_side=True, lower=True)
    attn = jnp.einsum('nbhid,nbhjd->nbhij', qc, kc) * decay_incl
    q_g = qc * jnp.exp(gc)[..., None]
    g_last = gc[..., -1]
    k_tail = kc * jnp.exp(g_last[..., None] - gc)[..., None]

    def step(s, xs):
        value_c, kcum_c, attn_c, qg_c, kt_c, gl_c = xs
        v_new = value_c - jnp.einsum('bhcd,bhde->bhce', kcum_c, s)
        o_c = jnp.einsum('bhcd,bhde->bhce', qg_c, s) + jnp.einsum('bhij,bhje->bhie', attn_c, v_new)
        s = s * jnp.exp(gl_c)[..., None, None] + jnp.einsum('bhcd,bhce->bhde', kt_c, v_new)
        return s, o_c

    s_fin, o = lax.scan(step, s0.astype(F32), (value, kcum, attn, q_g, k_tail, g_last))
    o = o.transpose(1, 0, 3, 2, 4).reshape(b, n * c, h, dv)[:, :t]
    return o, s_fin


def _deltanet_core(proj_main, proj_tail, conv_buf, s0, w_conv, a_log, dt_bias, norm_w):
    b, t, _ = proj_main.shape
    qkv, z = proj_main[..., :DN_CONV_DIM], proj_main[..., DN_CONV_DIM:]
    beta_logit, a_in = proj_tail[..., :DN_V_HEADS], proj_tail[..., DN_V_HEADS:]
    qkv, new_buf = _causal_short_conv(qkv, conv_buf, w_conv)
    q, k, v = jnp.split(qkv, [DN_KEY_DIM, 2 * DN_KEY_DIM], axis=-1)
    rep = DN_V_HEADS // DN_QK_HEADS
    q = jnp.repeat(_l2_normalize(q.reshape(b, t, DN_QK_HEADS, DN_HEAD_DIM)), rep, axis=2) * DN_HEAD_DIM ** -0.5
    k = jnp.repeat(_l2_normalize(k.reshape(b, t, DN_QK_HEADS, DN_HEAD_DIM)), rep, axis=2)
    v = v.reshape(b, t, DN_V_HEADS, DN_HEAD_DIM)
    beta = jax.nn.sigmoid(beta_logit)
    log_decay = -jnp.exp(a_log) * jax.nn.softplus(a_in + dt_bias)
    o, s_new = _gated_delta_chunked(q, k, v, log_decay, beta, s0)
    o = o * lax.rsqrt(jnp.mean(o * o, axis=-1, keepdims=True) + NORM_EPS) * norm_w
    o = o * jax.nn.silu(z.reshape(b, t, DN_V_HEADS, DN_HEAD_DIM))
    return o.reshape(b * t, DN_VAL_DIM), new_buf, s_new


def _alibi_slopes():
    h = jnp.arange(1, N_ATT_HEADS + 1, dtype=F32)
    return (2.0 ** (-ALIBI_MAX_BIAS * h / N_ATT_HEADS)).reshape(N_GROUPS, HEADS_PER_GROUP)


def _split_kv_groups(kv):
    b, t, _ = kv.shape
    kv = kv.reshape(b, t, 2, N_GROUPS, HEADS_PER_GROUP, ATT_HEAD_DIM)
    return [kv[:, :, :, g] for g in range(N_GROUPS)]


def _dilated_attention_prompt(q, k, v, window, dilation, slopes):
    b, s, h, hd = q.shape
    d = dilation
    n = s // d
    w = window // d
    blk = w
    nb = -(-n // blk)
    npad = nb * blk

    def stride_gather(t):
        return t.reshape(b, n, d, h, hd).transpose(0, 2, 1, 3, 4).reshape(b * d, n, h, hd)

    qs = jnp.pad(stride_gather(q), ((0, 0), (0, npad - n), (0, 0), (0, 0))).reshape(b * d, nb, blk, h, hd)

    def band(t):
        t = jnp.pad(stride_gather(t), ((0, 0), (blk, npad - n), (0, 0), (0, 0))).reshape(b * d, nb + 1, blk, h, hd)
        return jnp.concatenate([t[:, :-1], t[:, 1:]], axis=2)

    k_band, v_band = band(k), band(v)
    sc = jnp.einsum('znqhd,znkhd->znhqk', qs, k_band, preferred_element_type=F32) * hd ** -0.5
    qpos = jnp.arange(nb)[:, None] * blk + jnp.arange(blk)[None, :]
    kpos = jnp.arange(nb)[:, None] * blk - blk + jnp.arange(2 * blk)[None, :]
    dist = qpos[:, :, None] - kpos[:, None, :]
    valid = (dist >= 0) & (dist <= w) & (kpos[:, None, :] >= 0)
    bias = -slopes[None, :, None, None] * (dist * d).astype(F32)[:, None]
    sc = jnp.where(valid[:, None], sc + bias, -jnp.inf)
    lse = jax.nn.logsumexp(sc, axis=-1)
    p = jnp.exp(sc - lse[..., None])
    o = jnp.einsum('znhqk,znkhd->znqhd', p, v_band).reshape(b * d, npad, h, hd)[:, :n]
    o = o.reshape(b, d, n, h, hd).transpose(0, 2, 1, 3, 4).reshape(b, s, h, hd)
    lse = lse.transpose(0, 1, 3, 2).reshape(b * d, npad, h)[:, :n]
    lse = lse.reshape(b, d, n, h).transpose(0, 2, 1, 3).reshape(b, s, h)
    return o, lse


def _dilated_attention_sample(q, kv_ext, window, dilation, slopes):
    t = q.shape[1]
    l = kv_ext.shape[1] - t
    hd = q.shape[-1]
    j = jnp.arange(window // dilation + 1)
    idx = (l + jnp.arange(t))[:, None] - dilation * j[None, :]
    valid = idx >= 0
    kv_g = jnp.take(kv_ext, jnp.maximum(idx, 0), axis=1)
    sc = jnp.einsum('bthd,btjhd->bhtj', q, kv_g[:, :, :, 0], preferred_element_type=F32) * hd ** -0.5
    sc = sc - slopes[None, :, None, None] * (dilation * j).astype(F32)[None, None, None, :]
    sc = jnp.where(valid[None, None], sc, -jnp.inf)
    lse = jax.nn.logsumexp(sc, axis=-1)
    p = jnp.exp(sc - lse[..., None])
    o = jnp.einsum('bhtj,btjhd->bthd', p, kv_g[:, :, :, 1])
    return o, lse.transpose(0, 2, 1)


def _merge_groups(outs, lses):
    wts = jax.nn.softmax(jnp.stack(lses, axis=0), axis=0)
    return jnp.einsum('gbth,gbthd->bthd', wts, jnp.stack(outs, axis=0))


def kernel(x_prompt, x_sample, state_dn_S, state_dn_conv, cache_kv_w128, cache_kv_w512, cache_kv_w2048,
           w_dn_in, w_dn_conv, dn_a_log, dn_dt_bias, dn_norm_w, w_dn_out, w_kv_shared, w_att_q, w_att_out,
           ln_g, ln_b, w_router, b_router, w_up, b_up, w_down, b_down):
    bp, sp, _ = x_prompt.shape
    bs, ts, _ = x_sample.shape
    n_p = bp * sp
    slopes = _alibi_slopes()
    kv_caches = (cache_kv_w128, cache_kv_w512, cache_kv_w2048)
    x = jnp.concatenate([x_prompt.reshape(n_p, D_MODEL), x_sample.reshape(bs * ts, D_MODEL)], axis=0)

    def moe_block(x, layer):
        ffn = _moe_ffn(x, w_router, b_router, w_up, b_up, w_down, b_down, layer)
        return _post_norm(x, ffn, ln_g[layer, 1], ln_b[layer, 1])

    w_in = w_dn_in.reshape(D_MODEL, DN_IN_DIM)
    x_bf = x.astype(BF16)
    n_main = DN_CONV_DIM + DN_VAL_DIM
    proj_main = _dense(x_bf, w_in, n_cols=n_main)
    proj_tail = _dense(x_bf, w_in, col_block_offset=n_main // LANES, n_cols=LANES, tn=LANES)[:, :2 * DN_V_HEADS]
    conv_zero = jnp.zeros((bp, DN_CONV_W - 1, DN_CONV_DIM), F32)
    s_zero = jnp.zeros((bp, DN_V_HEADS, DN_HEAD_DIM, DN_HEAD_DIM), F32)
    o_p, cbuf_p, s_p = _deltanet_core(proj_main[:n_p].reshape(bp, sp, n_main), proj_tail[:n_p].reshape(bp, sp, -1),
                                      conv_zero, s_zero, w_dn_conv[0], dn_a_log[0], dn_dt_bias[0], dn_norm_w[0])
    o_s, cbuf_s, s_s = _deltanet_core(proj_main[n_p:].reshape(bs, ts, n_main), proj_tail[n_p:].reshape(bs, ts, -1),
                                      state_dn_conv[0], state_dn_S[0], w_dn_conv[0], dn_a_log[0], dn_dt_bias[0],
                                      dn_norm_w[0])
    o = jnp.concatenate([o_p, o_s], axis=0).astype(BF16)
    mix = _dense(o, w_dn_out.reshape(DN_VAL_DIM, D_MODEL))
    x = _post_norm(x, mix, ln_g[0, 0], ln_b[0, 0])
    x = moe_block(x, 0)

    x_bf = x.astype(BF16)
    kv = _dense(x_bf, w_kv_shared)
    kv_p = _split_kv_groups(kv[:n_p].reshape(bp, sp, -1))
    kv_s_new = _split_kv_groups(kv[n_p:].reshape(bs, ts, -1))
    kv_ext_s = [jnp.concatenate([kv_caches[g], kv_s_new[g]], axis=1) for g in range(N_GROUPS)]
    new_kv_p = [kv_p[g][:, -min(GROUP_WINDOWS[g], sp):] for g in range(N_GROUPS)]
    new_kv_s = [kv_ext_s[g][:, ts:] for g in range(N_GROUPS)]

    q = _dense(x_bf, w_att_q.reshape(D_MODEL, N_ATT_HEADS * ATT_HEAD_DIM))
    q_p = q[:n_p].reshape(bp, sp, N_GROUPS, HEADS_PER_GROUP, ATT_HEAD_DIM)
    q_s = q[n_p:].reshape(bs, ts, N_GROUPS, HEADS_PER_GROUP, ATT_HEAD_DIM)
    outs_p, lses_p, outs_s, lses_s = [], [], [], []
    for g in range(N_GROUPS):
        o_g, lse_g = _dilated_attention_prompt(q_p[:, :, g], kv_p[g][:, :, 0], kv_p[g][:, :, 1],
                                               GROUP_WINDOWS[g], GROUP_DILATIONS[g], slopes[g])
        outs_p.append(o_g)
        lses_p.append(lse_g)
        o_g, lse_g = _dilated_attention_sample(q_s[:, :, g], kv_ext_s[g], GROUP_WINDOWS[g], GROUP_DILATIONS[g],
                                               slopes[g])
        outs_s.append(o_g)
        lses_s.append(lse_g)
    att = jnp.concatenate([_merge_groups(outs_p, lses_p).reshape(n_p, ATT_SLOT_DIM),
                           _merge_groups(outs_s, lses_s).reshape(bs * ts, ATT_SLOT_DIM)], axis=0).astype(BF16)
    mix = _dense(att, w_att_out.reshape(ATT_SLOT_DIM, D_MODEL))
    x = _post_norm(x, mix, ln_g[1, 0], ln_b[1, 0])
    x = moe_block(x, 1)

    y_prompt = x[:n_p].reshape(bp, sp, D_MODEL)
    y_sample = x[n_p:].reshape(bs, ts, D_MODEL)
    return (y_prompt, y_sample, s_p[None], cbuf_p[None], new_kv_p[0], new_kv_p[1], new_kv_p[2],
            s_s[None], cbuf_s[None], new_kv_s[0], new_kv_s[1], new_kv_s[2])
```

```python
import functools
import math

import jax
import jax.numpy as jnp
from jax import lax
from jax.experimental import pallas as pl
from jax.experimental.pallas import tpu as pltpu

D_MODEL = 4096
DEPTH = 2
N_A_LAYERS = DEPTH // 2

DN_QK_HEADS = 16
DN_V_HEADS = 32
DN_HEAD_DIM = 128
DN_KEY_DIM = DN_QK_HEADS * DN_HEAD_DIM
DN_VAL_DIM = DN_V_HEADS * DN_HEAD_DIM
DN_CONV_W = 4
DN_CONV_DIM = 2 * DN_KEY_DIM + DN_VAL_DIM
DN_IN_DIM = DN_CONV_DIM + DN_VAL_DIM + 2 * DN_V_HEADS
DN_CHUNK = 64

GROUP_WINDOWS = (128, 512, 2048)
GROUP_DILATIONS = (1, 4, 16)
N_GROUPS = 3
HEADS_PER_GROUP = 16
ATT_HEAD_DIM = 128
N_ATT_HEADS = N_GROUPS * HEADS_PER_GROUP
ATT_SLOT_DIM = HEADS_PER_GROUP * ATT_HEAD_DIM
ALIBI_MAX_BIAS = 8.0

N_EXPERTS = 32
TOP_K = 4
D_EXPERT = D_MODEL // 2
SWIGLU_LIMIT = 7.0
SWIGLU_ALPHA = 1.702

LN_EPS = 1e-5
NORM_EPS = 1e-6
DEEP_ALPHA = (2.0 * DEPTH) ** 0.25

LANES = 128
SUBLANES = 8
MXU_DIM = 256
VMEM_LIMIT_BYTES = 56 * 1024 * 1024

DENSE_TM = 688
DENSE_TN = 512
MOE_TM = 256
MOE_UP_TN = 512
MOE_DOWN_TN = 512

BF16 = jnp.bfloat16
F32 = jnp.float32


def _dense_kernel(x_ref, w_ref, o_ref, wbf_ref):
    @pl.when(pl.program_id(1) == 0)
    def _():
        wbf_ref[...] = w_ref[...].astype(BF16)

    o_ref[...] = jnp.dot(x_ref[...], wbf_ref[...], preferred_element_type=F32).astype(o_ref.dtype)


def _dense(x, w, *, col_block_offset=0, n_cols=None, tn=DENSE_TN, out_dtype=F32):
    m, k = x.shape
    n_cols = w.shape[1] if n_cols is None else n_cols
    tm = DENSE_TM if m % DENSE_TM == 0 else m
    assert m % tm == 0 and n_cols % tn == 0
    return pl.pallas_call(
        _dense_kernel,
        grid=(n_cols // tn, m // tm),
        in_specs=[pl.BlockSpec((tm, k), lambda j, i: (i, 0)),
                  pl.BlockSpec((k, tn), lambda j, i: (0, j + col_block_offset))],
        out_specs=pl.BlockSpec((tm, tn), lambda j, i: (i, j)),
        out_shape=jax.ShapeDtypeStruct((m, n_cols), out_dtype),
        scratch_shapes=[pltpu.VMEM((k, tn), BF16)],
        compiler_params=pltpu.CompilerParams(dimension_semantics=("arbitrary", "arbitrary"),
                                             vmem_limit_bytes=VMEM_LIMIT_BYTES),
        name="dense_proj",
    )(x, w)


def _moe_rows(n_tokens):
    n_assign = n_tokens * TOP_K
    n_tiles = -(-(n_assign + N_EXPERTS * (MOE_TM - 1)) // MOE_TM)
    return n_tiles, n_tiles * MOE_TM


def _route(logits):
    n_tokens = logits.shape[0]
    n_tiles, n_rows = _moe_rows(n_tokens)
    top_vals, top_idx = lax.top_k(logits, TOP_K)
    gates = jax.nn.softmax(top_vals, axis=-1).reshape(-1)
    flat_e = top_idx.reshape(-1).astype(jnp.int32)
    onehot = (flat_e[:, None] == jnp.arange(N_EXPERTS, dtype=jnp.int32)[None, :]).astype(jnp.int32)
    csum = jnp.cumsum(onehot, axis=0)
    rank = jnp.take_along_axis(csum, flat_e[:, None], axis=1)[:, 0] - 1
    counts = csum[-1]
    padded = ((counts + MOE_TM - 1) // MOE_TM) * MOE_TM
    pend = jnp.cumsum(padded)
    pstart = pend - padded
    row_of = pstart[flat_e] + rank
    src_tok = jnp.zeros((n_rows,), jnp.int32).at[row_of].set(jnp.arange(flat_e.shape[0], dtype=jnp.int32) // TOP_K)
    gate_row = jnp.zeros((n_rows,), F32).at[row_of].set(gates)
    n_used = (pend[-1] // MOE_TM).astype(jnp.int32)
    tile_start = jnp.minimum(jnp.arange(n_tiles, dtype=jnp.int32), n_used - 1) * MOE_TM
    tile_expert = jnp.minimum(jnp.searchsorted(pend, tile_start, side="right"), N_EXPERTS - 1).astype(jnp.int32)
    return row_of, src_tok, gate_row, tile_expert, n_used.reshape(1)


def _weights_changed(te_ref, i):
    return (i == 0) | (te_ref[i] != te_ref[jnp.maximum(i - 1, 0)])


def _moe_up_kernel(te_ref, nu_ref, x_ref, w_ref, b_ref, o_ref, wbf_ref):
    i = pl.program_id(1)

    @pl.when(_weights_changed(te_ref, i))
    def _():
        wbf_ref[...] = w_ref[0, 0].astype(BF16)

    @pl.when(i < nu_ref[0])
    def _():
        h = jnp.dot(x_ref[...], wbf_ref[...], preferred_element_type=F32) + b_ref[0]
        glu = jnp.minimum(h, SWIGLU_LIMIT)
        glu = glu * jax.nn.sigmoid(SWIGLU_ALPHA * glu)
        lin = jnp.clip(h, -SWIGLU_LIMIT, SWIGLU_LIMIT) + 1.0
        tn = h.shape[1]
        prod = glu * pltpu.roll(lin, tn - 1, 1)
        lane = lax.broadcasted_iota(jnp.int32, prod.shape, 1)
        prod = jnp.where(lane % 2 == 0, prod, 0.0).astype(BF16)
        r = lax.broadcasted_iota(jnp.int32, (MXU_DIM, MXU_DIM // 2), 0)
        c = lax.broadcasted_iota(jnp.int32, (MXU_DIM, MXU_DIM // 2), 1)
        sel = (r == 2 * c).astype(BF16)
        for q in range(tn // MXU_DIM):
            part = jnp.dot(prod[:, q * MXU_DIM:(q + 1) * MXU_DIM], sel, preferred_element_type=F32)
            o_ref[:, q * (MXU_DIM // 2):(q + 1) * (MXU_DIM // 2)] = part.astype(o_ref.dtype)


def _moe_down_kernel(te_ref, nu_ref, h_ref, w_ref, b_ref, g_ref, o_ref, wbf_ref):
    i = pl.program_id(1)

    @pl.when(_weights_changed(te_ref, i))
    def _():
        wbf_ref[...] = w_ref[0, 0].astype(BF16)

    @pl.when(i < nu_ref[0])
    def _():
        y = jnp.dot(h_ref[...], wbf_ref[...], preferred_element_type=F32) + b_ref[0]
        o_ref[...] = y * g_ref[...]


def _moe_experts(x_rows, gate_row, tile_expert, n_used, w_up, b_up, w_down, b_down, layer):
    n_rows = x_rows.shape[0]
    n_tiles = n_rows // MOE_TM
    tm = MOE_TM

    def row_tile(j, i, te, nu):
        return (jnp.minimum(i, nu[0] - 1), 0)

    tn = MOE_UP_TN
    hact = pl.pallas_call(
        _moe_up_kernel,
        grid_spec=pltpu.PrefetchScalarGridSpec(
            num_scalar_prefetch=2,
            grid=(2 * D_EXPERT // tn, n_tiles),
            in_specs=[pl.BlockSpec((tm, D_MODEL), row_tile),
                      pl.BlockSpec((1, 1, D_MODEL, tn), lambda j, i, te, nu: (layer, te[i], 0, j)),
                      pl.BlockSpec((1, 1, tn), lambda j, i, te, nu: (layer * N_EXPERTS + te[i], 0, j))],
            out_specs=pl.BlockSpec((tm, tn // 2), lambda j, i, te, nu: (jnp.minimum(i, nu[0] - 1), j)),
            scratch_shapes=[pltpu.VMEM((D_MODEL, tn), BF16)]),
        out_shape=jax.ShapeDtypeStruct((n_rows, D_EXPERT), BF16),
        compiler_params=pltpu.CompilerParams(dimension_semantics=("arbitrary", "arbitrary"),
                                             vmem_limit_bytes=VMEM_LIMIT_BYTES),
        name="moe_up_swiglu",
    )(tile_expert, n_used, x_rows, w_up, b_up.reshape(DEPTH * N_EXPERTS, 1, 2 * D_EXPERT))

    tn = MOE_DOWN_TN
    return pl.pallas_call(
        _moe_down_kernel,
        grid_spec=pltpu.PrefetchScalarGridSpec(
            num_scalar_prefetch=2,
            grid=(D_MODEL // tn, n_tiles),
            in_specs=[pl.BlockSpec((tm, D_EXPERT), row_tile),
                      pl.BlockSpec((1, 1, D_EXPERT, tn), lambda j, i, te, nu: (layer, te[i], 0, j)),
                      pl.BlockSpec((1, 1, tn), lambda j, i, te, nu: (layer * N_EXPERTS + te[i], 0, j)),
                      pl.BlockSpec((tm, 1), row_tile)],
            out_specs=pl.BlockSpec((tm, tn), lambda j, i, te, nu: (jnp.minimum(i, nu[0] - 1), j)),
            scratch_shapes=[pltpu.VMEM((D_EXPERT, tn), BF16)]),
        out_shape=jax.ShapeDtypeStruct((n_rows, D_MODEL), F32),
        compiler_params=pltpu.CompilerParams(dimension_semantics=("arbitrary", "arbitrary"),
                                             vmem_limit_bytes=VMEM_LIMIT_BYTES),
        name="moe_down",
    )(tile_expert, n_used, hact, w_down, b_down.reshape(DEPTH * N_EXPERTS, 1, D_MODEL), gate_row.reshape(n_rows, 1))


DN_VH_STEP = 8
DN_QKH_STEP = DN_VH_STEP // (DN_V_HEADS // DN_QK_HEADS)
DN_GROUPS = DN_V_HEADS // DN_VH_STEP
DN_BASE_BLOCK = 16


def _dot(a, b):
    return jnp.dot(a, b, preferred_element_type=F32)


def _dot_nt(a, b):
    return lax.dot_general(a, b, (((1,), (1,)), ((), ())), preferred_element_type=F32)


def _split(a):
    hi = a.astype(BF16)
    return hi, (a - hi.astype(F32)).astype(BF16)


def _dot3(a, b, dot=_dot):
    return dot(a[0], b[0]) + (dot(a[0], b[1]) + dot(a[1], b[0]))


def _unit_lower_inverse(low, ii, jj):
    eye = (ii == jj).astype(F32)
    blk16 = (ii // DN_BASE_BLOCK) == (jj // DN_BASE_BLOCK)
    blk32 = (ii // (2 * DN_BASE_BLOCK)) == (jj // (2 * DN_BASE_BLOCK))
    x = jnp.where(blk16, -low, 0.0)
    t = eye + x
    p = x
    for _ in range(3):
        ps = _split(p)
        p = _dot3(ps, ps)
        t = t + _dot3(_split(t), _split(p))
    for off in (jnp.where(blk32 & ~blk16, low, 0.0), jnp.where(~blk32, low, 0.0)):
        ts = _split(t)
        t = t - _dot3(ts, _split(_dot3(_split(off), ts)))
    return t


def _dn_kernel(q_ref, k_ref, v_ref, z_ref, gate_ref, cq_ref, ck_ref, cv_ref, s0_ref, wq_ref, wk_ref, wv_ref,
               hp_ref, nw_ref, o_ref, s_ref, pq_ref, pk_ref, pv_ref, *, n_valid):
    c = DN_CHUNK
    hd = DN_HEAD_DIM

    @pl.when(pl.program_id(2) == 0)
    def _():
        pq_ref[c - SUBLANES:c, :] = cq_ref[0]
        pk_ref[c - SUBLANES:c, :] = ck_ref[0]
        pv_ref[c - SUBLANES:c, :] = cv_ref[0]
        s_ref[...] = s0_ref[...]

    row = lax.broadcasted_iota(jnp.int32, (c, 1), 0)
    ii = lax.broadcasted_iota(jnp.int32, (c, c), 0)
    jj = lax.broadcasted_iota(jnp.int32, (c, c), 1)

    def conv_silu(u_ref, prev_ref, w_ref):
        u = u_ref[...]
        prev = prev_ref[...]
        y = u * w_ref[DN_CONV_W - 1:DN_CONV_W, :]
        for s in range(1, DN_CONV_W):
            shifted = jnp.where(row >= s, pltpu.roll(u, s, 0), pltpu.roll(prev, s, 0))
            y = y + shifted * w_ref[DN_CONV_W - 1 - s:DN_CONV_W - s, :]
        prev_ref[...] = u
        return y * jax.nn.sigmoid(y)

    q_all = conv_silu(q_ref, pq_ref, wq_ref)
    k_all = conv_silu(k_ref, pk_ref, wk_ref)
    v_all = conv_silu(v_ref, pv_ref, wv_ref)

    gt = gate_ref[...]
    beta_all = jax.nn.sigmoid(gt)
    pre = gt + hp_ref[0, 0:1, :]
    softplus = jnp.maximum(pre, 0.0) + jnp.log1p(jnp.exp(-jnp.abs(pre)))
    g_all = -jnp.exp(hp_ref[0, 1:2, :]) * softplus
    if n_valid < c:
        live = row < n_valid
        beta_all = jnp.where(live, beta_all, 0.0)
        g_all = jnp.where(live, g_all, 0.0)
        k_all = jnp.where(live, k_all, 0.0)
        v_all = jnp.where(live, v_all, 0.0)
    gc = g_all
    s = 1
    while s < c:
        gc = gc + jnp.where(row >= s, pltpu.roll(gc, s, 0), 0.0)
        s *= 2
    gc_t = gc.T

    def l2n(t):
        return t * lax.rsqrt(jnp.sum(t * t, axis=-1, keepdims=True) + NORM_EPS)

    for j in range(DN_QKH_STEP):
        q_j = l2n(q_all[:, j * hd:(j + 1) * hd]) * hd ** -0.5
        k_j = l2n(k_all[:, j * hd:(j + 1) * hd])
        k_s = _split(k_j)
        kk_qk = _dot3(_split(jnp.concatenate([k_j, q_j], axis=0)), k_s, _dot_nt)
        kk, qk = kk_qk[:c], kk_qk[c:]
        for hv in range(j * (DN_VH_STEP // DN_QKH_STEP), (j + 1) * (DN_VH_STEP // DN_QKH_STEP)):
            g_col = gc[:, SUBLANES + hv:SUBLANES + hv + 1]
            g_row = gc_t[SUBLANES + hv:SUBLANES + hv + 1, :]
            beta = beta_all[:, hv:hv + 1]
            decay = jnp.exp(jnp.minimum(g_col - g_row, 0.0))
            low = jnp.where(ii > jj, beta * kk * decay, 0.0)
            t_inv = _unit_lower_inverse(low, ii, jj)
            e_col = jnp.exp(g_col)
            v_h = v_all[:, hv * hd:(hv + 1) * hd]
            rhs = jnp.concatenate([v_h * beta, k_j * (beta * e_col)], axis=1)
            w = _dot3(_split(t_inv), _split(rhs))
            value, kcum = w[:, :hd], w[:, hd:]
            attn = jnp.where(ii >= jj, qk * decay, 0.0)
            g_last = g_col[c - 1:c, :]
            k_tail = k_j * jnp.exp(g_last - g_col)
            state = s_ref[0, hv]
            ks_qs = _dot3(_split(jnp.concatenate([kcum, q_j * e_col], axis=0)), _split(state))
            v_new = value - ks_qs[:c]
            o = ks_qs[c:] + _dot(attn.astype(BF16), v_new.astype(BF16))
            s_ref[0, hv] = state * jnp.exp(g_last) + _dot3(_split(k_tail.T), _split(v_new))
            o = o * lax.rsqrt(jnp.mean(o * o, axis=-1, keepdims=True) + NORM_EPS) * nw_ref[...]
            z_h = z_ref[:, hv * hd:(hv + 1) * hd]
            o_ref[:, hv * hd:(hv + 1) * hd] = (o * (z_h * jax.nn.sigmoid(z_h))).astype(o_ref.dtype)


def _deltanet(proj, gates, conv_init, s0, w_conv, head_params, norm_w, *, batch, n_chunks, n_valid):
    nc = n_chunks
    qw = DN_QKH_STEP * DN_HEAD_DIM
    vw = DN_VH_STEP * DN_HEAD_DIM
    k_off = DN_KEY_DIM // qw
    v_off = 2 * DN_KEY_DIM // vw
    z_off = DN_CONV_DIM // vw
    chunk = lambda width, off: pl.BlockSpec((DN_CHUNK, width), lambda bi, g, ci: (bi * nc + ci, off + g))
    init = lambda width, off: pl.BlockSpec((1, SUBLANES, width), lambda bi, g, ci: (bi, 0, off + g))
    wspec = lambda width, off: pl.BlockSpec((DN_CONV_W, width), lambda bi, g, ci: (0, off + g))
    state = pl.BlockSpec((1, DN_VH_STEP, DN_HEAD_DIM, DN_HEAD_DIM), lambda bi, g, ci: (bi, g, 0, 0))
    return pl.pallas_call(
        functools.partial(_dn_kernel, n_valid=n_valid),
        grid=(batch, DN_GROUPS, nc),
        in_specs=[chunk(qw, 0), chunk(qw, k_off), chunk(vw, v_off), chunk(vw, z_off), chunk(LANES, 0),
                  init(qw, 0), init(qw, k_off), init(vw, v_off), state,
                  wspec(qw, 0), wspec(qw, k_off), wspec(vw, v_off),
                  pl.BlockSpec((1, SUBLANES, LANES), lambda bi, g, ci: (g, 0, 0)),
                  pl.BlockSpec((1, DN_HEAD_DIM), lambda bi, g, ci: (0, 0))],
        out_specs=[chunk(vw, 0), state],
        out_shape=[jax.ShapeDtypeStruct((batch * nc * DN_CHUNK, DN_VAL_DIM), BF16),
                   jax.ShapeDtypeStruct((batch, DN_V_HEADS, DN_HEAD_DIM, DN_HEAD_DIM), F32)],
        scratch_shapes=[pltpu.VMEM((DN_CHUNK, qw), F32), pltpu.VMEM((DN_CHUNK, qw), F32),
                        pltpu.VMEM((DN_CHUNK, vw), F32)],
        compiler_params=pltpu.CompilerParams(dimension_semantics=("arbitrary", "arbitrary", "arbitrary"),
                                             vmem_limit_bytes=VMEM_LIMIT_BYTES),
        name="deltanet_chunk",
    )(proj, proj, proj, proj, gates, conv_init, conv_init, conv_init, s0, w_conv, w_conv, w_conv,
      head_params, norm_w.reshape(1, DN_HEAD_DIM))


def _dn_gate_layout(tail):
    lead = tail.shape[:-1]
    t = tail.reshape(lead + (2, DN_GROUPS, DN_VH_STEP))
    t = jnp.swapaxes(t, -3, -2).reshape(lead + (DN_GROUPS, 2 * DN_VH_STEP))
    t = jnp.pad(t, [(0, 0)] * (len(lead) + 1) + [(0, LANES - 2 * DN_VH_STEP)])
    return t.reshape(lead + (DN_GROUPS * LANES,))


def _dn_head_params(a_log, dt_bias):
    def lay(p):
        p = p.reshape(DN_GROUPS, 1, DN_VH_STEP)
        return jnp.pad(p, ((0, 0), (0, 0), (DN_VH_STEP, LANES - 2 * DN_VH_STEP)))
    rows = jnp.concatenate([lay(dt_bias), lay(a_log)], axis=1)
    return jnp.pad(rows, ((0, 0), (0, SUBLANES - 2), (0, 0)))


ATT_BLK = 128
ATT_MASKED = -1e30


def _att_prompt_kernel(slope_ref, q_ref, kc_ref, kp_ref, vc_ref, vp_ref, o_ref, lse_ref, *, group, dilation):
    hd = ATT_HEAD_DIM
    blk = ATT_BLK
    qi = lax.broadcasted_iota(jnp.int32, (blk, 2 * blk), 0)
    kj = lax.broadcasted_iota(jnp.int32, (blk, 2 * blk), 1)
    dist = qi + blk - kj
    valid = (dist >= 0) & (dist <= blk) & ((kj >= blk) | (pl.program_id(2) > 0))
    dist_f = (dist * dilation).astype(F32)
    lse_ref[...] = jnp.zeros(lse_ref.shape, F32)
    for h in range(HEADS_PER_GROUP):
        hs = slice(h * hd, (h + 1) * hd)
        q = q_ref[:, hs].astype(BF16)
        k = jnp.concatenate([kp_ref[:, hs], kc_ref[:, hs]], axis=0).astype(BF16)
        v = jnp.concatenate([vp_ref[:, hs], vc_ref[:, hs]], axis=0).astype(BF16)
        sc = _dot_nt(q, k) * hd ** -0.5 - slope_ref[group, h] * dist_f
        sc = jnp.where(valid, sc, ATT_MASKED)
        m = jnp.max(sc, axis=-1, keepdims=True)
        p = jnp.exp(sc - m)
        l = jnp.sum(p, axis=-1, keepdims=True)
        o_ref[:, hs] = _dot(p.astype(BF16), v) / l
        lse_ref[:, h:h + 1] = m + jnp.log(l)


def _att_prompt(q, kv, slopes, *, group, batch, seq):
    d = GROUP_DILATIONS[group]
    assert GROUP_WINDOWS[group] // d == ATT_BLK and seq % (d * ATT_BLK) == 0 and q.shape[0] % d == 0
    nb = seq // d // ATT_BLK
    n_q = q.shape[1] // ATT_SLOT_DIM
    n_kv = kv.shape[1] // ATT_SLOT_DIM
    qv = q.reshape(q.shape[0] // d, d * q.shape[1])
    kvv = kv.reshape(kv.shape[0] // d, d * kv.shape[1])
    cur = lambda per_row, col: pl.BlockSpec((ATT_BLK, ATT_SLOT_DIM), lambda b, r, n: (b * nb + n, r * per_row + col))
    prev = lambda per_row, col: pl.BlockSpec((ATT_BLK, ATT_SLOT_DIM),
                                             lambda b, r, n: (b * nb + jnp.maximum(n - 1, 0), r * per_row + col))
    o, lse = pl.pallas_call(
        functools.partial(_att_prompt_kernel, group=group, dilation=d),
        grid=(batch, d, nb),
        in_specs=[pl.BlockSpec(memory_space=pltpu.SMEM),
                  cur(n_q, group), cur(n_kv, group), prev(n_kv, group),
                  cur(n_kv, n_kv // 2 + group), prev(n_kv, n_kv // 2 + group)],
        out_specs=[cur(1, 0), pl.BlockSpec((ATT_BLK, LANES), lambda b, r, n: (b * nb + n, r))],
        out_shape=[jax.ShapeDtypeStruct((batch * seq // d, d * ATT_SLOT_DIM), F32),
                   jax.ShapeDtypeStruct((batch * seq // d, d * LANES), F32)],
        compiler_params=pltpu.CompilerParams(dimension_semantics=("arbitrary", "arbitrary", "arbitrary"),
                                             vmem_limit_bytes=VMEM_LIMIT_BYTES),
        name="dilated_attention_prompt",
    )(slopes, qv, kvv, kvv, kvv, kvv)
    return o.reshape(batch * seq, ATT_SLOT_DIM), lse.reshape(batch * seq, LANES)


ATT_S_HEADS = 4


def _att_sample_kernel(slope_ref, q_ref, kc_ref, vc_ref, kn_ref, vn_ref, o_ref, lse_ref, *, group, dilation):
    hd = ATT_HEAD_DIM
    t_new = q_ref.shape[1]
    l_buf = kc_ref.shape[1]
    window = GROUP_WINDOWS[group]
    tq = lax.broadcasted_iota(jnp.int32, (t_new, l_buf), 0)
    ic = lax.broadcasted_iota(jnp.int32, (t_new, l_buf), 1)
    dist_c = l_buf + tq - ic
    ok_c = (((ic - tq) & (dilation - 1)) == 0) & (dist_c <= window)
    tn = lax.broadcasted_iota(jnp.int32, (t_new, t_new), 0)
    un = lax.broadcasted_iota(jnp.int32, (t_new, t_new), 1)
    dist_n = tn - un
    ok_n = (dist_n >= 0) & ((dist_n & (dilation - 1)) == 0) & (dist_n <= window)
    lane = lax.broadcasted_iota(jnp.int32, (t_new, LANES), 1)
    lse_all = jnp.zeros((t_new, LANES), F32)
    for h in range(ATT_S_HEADS):
        hs = slice(h * hd, (h + 1) * hd)
        slope = slope_ref[group, pl.program_id(1) * ATT_S_HEADS + h]
        q = q_ref[0, :, hs].astype(BF16)
        sc_c = _dot_nt(q, kc_ref[0, :, hs].astype(BF16)) * hd ** -0.5 - slope * dist_c.astype(F32)
        sc_n = _dot_nt(q, kn_ref[0, :, hs].astype(BF16)) * hd ** -0.5 - slope * dist_n.astype(F32)
        sc_c = jnp.where(ok_c, sc_c, ATT_MASKED)
        sc_n = jnp.where(ok_n, sc_n, ATT_MASKED)
        m = jnp.maximum(jnp.max(sc_c, axis=-1, keepdims=True), jnp.max(sc_n, axis=-1, keepdims=True))
        p_c = jnp.exp(sc_c - m)
        p_n = jnp.exp(sc_n - m)
        l = jnp.sum(p_c, axis=-1, keepdims=True) + jnp.sum(p_n, axis=-1, keepdims=True)
        acc = (_dot(p_c.astype(BF16), vc_ref[0, :, hs].astype(BF16))
               + _dot(p_n.astype(BF16), vn_ref[0, :, hs].astype(BF16)))
        o_ref[0, :, hs] = acc / l
        lse_all = jnp.where(lane == h, m + jnp.log(l), lse_all)
    lse_ref[0, 0] = lse_all


def _att_sample(q, kv_new, cache, slopes, *, group):
    b, t, _ = q.shape
    l_buf = cache.shape[1]
    w = ATT_S_HEADS * ATT_HEAD_DIM
    per_group = ATT_SLOT_DIM // w
    n_q = q.shape[2] // w
    cache2 = cache.reshape(b, l_buf, 2 * ATT_SLOT_DIM)
    new = lambda col: pl.BlockSpec((1, t, w), lambda bi, hc: (bi, 0, col + hc))
    buf = lambda col: pl.BlockSpec((1, l_buf, w), lambda bi, hc: (bi, 0, col + hc))
    o, lse = pl.pallas_call(
        functools.partial(_att_sample_kernel, group=group, dilation=GROUP_DILATIONS[group]),
        grid=(b, per_group),
        in_specs=[pl.BlockSpec(memory_space=pltpu.SMEM), new(group * per_group), buf(0), buf(per_group),
                  new(group * per_group), new(n_q + group * per_group)],
        out_specs=[new(0), pl.BlockSpec((1, 1, t, LANES), lambda bi, hc: (bi, hc, 0, 0))],
        out_shape=[jax.ShapeDtypeStruct((b, t, ATT_SLOT_DIM), F32),
                   jax.ShapeDtypeStruct((b, per_group, t, LANES), F32)],
        compiler_params=pltpu.CompilerParams(dimension_semantics=("arbitrary", "arbitrary"),
                                             vmem_limit_bytes=VMEM_LIMIT_BYTES),
        name="dilated_attention_sample",
    )(slopes, q, cache2, cache2, kv_new, kv_new)
    lse = lse[..., :ATT_S_HEADS].transpose(0, 2, 1, 3).reshape(b * t, HEADS_PER_GROUP)
    return o.reshape(b * t, ATT_SLOT_DIM), jnp.pad(lse, ((0, 0), (0, LANES - HEADS_PER_GROUP)))


def _att_merge_kernel(o0_ref, o1_ref, o2_ref, l0_ref, l1_ref, l2_ref, out_ref):
    hd = ATT_HEAD_DIM
    lses = [l0_ref[...], l1_ref[...], l2_ref[...]]
    m = jnp.maximum(jnp.maximum(lses[0], lses[1]), lses[2])
    zs = [jnp.exp(l - m) for l in lses]
    inv = 1.0 / (zs[0] + zs[1] + zs[2])
    for h in range(HEADS_PER_GROUP):
        hs = slice(h * hd, (h + 1) * hd)
        acc = (zs[0][:, h:h + 1] * inv[:, h:h + 1]) * o0_ref[:, hs]
        acc = acc + (zs[1][:, h:h + 1] * inv[:, h:h + 1]) * o1_ref[:, hs]
        acc = acc + (zs[2][:, h:h + 1] * inv[:, h:h + 1]) * o2_ref[:, hs]
        out_ref[:, hs] = acc.astype(out_ref.dtype)


def _att_merge(outs, lses):
    n = outs[0].shape[0]
    tm = 256 if n % 256 == 0 else n
    ospec = pl.BlockSpec((tm, ATT_SLOT_DIM), lambda i: (i, 0))
    lspec = pl.BlockSpec((tm, LANES), lambda i: (i, 0))
    return pl.pallas_call(
        _att_merge_kernel,
        grid=(n // tm,),
        in_specs=[ospec] * 3 + [lspec] * 3,
        out_specs=ospec,
        out_shape=jax.ShapeDtypeStruct((n, ATT_SLOT_DIM), BF16),
        compiler_params=pltpu.CompilerParams(dimension_semantics=("arbitrary",), vmem_limit_bytes=VMEM_LIMIT_BYTES),
        name="attention_group_merge",
    )(*outs, *lses)


NORM_TM = 192


def _postnorm_kernel(x_ref, f_ref, g_ref, b_ref, *rest, with_router):
    y = DEEP_ALPHA * x_ref[...] + f_ref[...]
    mu = jnp.mean(y, axis=-1, keepdims=True)
    yc = y - mu
    var = jnp.mean(yc * yc, axis=-1, keepdims=True)
    out = yc * lax.rsqrt(var + LN_EPS) * g_ref[...] + b_ref[...]
    if with_router:
        wr_ref, br_ref, o_ref, obf_ref, lg_ref = rest
        lg_ref[...] = jnp.dot(out, wr_ref[...], precision=lax.Precision.HIGHEST,
                              preferred_element_type=F32) + br_ref[...]
    else:
        o_ref, obf_ref = rest
    o_ref[...] = out
    obf_ref[...] = out.astype(BF16)


def _postnorm(x, f, g, b, w_router=None, b_router=None):
    n, d = x.shape
    tm = NORM_TM if n % NORM_TM == 0 else n
    row = pl.BlockSpec((tm, d), lambda i: (i, 0))
    vec = pl.BlockSpec((1, d), lambda i: (0, 0))
    with_router = w_router is not None
    in_specs = [row, row, vec, vec]
    args = [x, f, g.reshape(1, d), b.reshape(1, d)]
    out_specs = [row, row]
    out_shape = [jax.ShapeDtypeStruct((n, d), F32), jax.ShapeDtypeStruct((n, d), BF16)]
    if with_router:
        in_specs += [pl.BlockSpec((d, N_EXPERTS), lambda i: (0, 0)), pl.BlockSpec((1, N_EXPERTS), lambda i: (0, 0))]
        args += [w_router, b_router.reshape(1, N_EXPERTS)]
        out_specs.append(pl.BlockSpec((tm, N_EXPERTS), lambda i: (i, 0)))
        out_shape.append(jax.ShapeDtypeStruct((n, N_EXPERTS), F32))
    return pl.pallas_call(
        functools.partial(_postnorm_kernel, with_router=with_router),
        grid=(n // tm,),
        in_specs=in_specs, out_specs=out_specs, out_shape=out_shape,
        compiler_params=pltpu.CompilerParams(dimension_semantics=("arbitrary",), vmem_limit_bytes=VMEM_LIMIT_BYTES),
        name="postnorm_router" if with_router else "postnorm",
    )(*args)


def _alibi_slopes():
    h = jnp.arange(1, N_ATT_HEADS + 1, dtype=F32)
    return (2.0 ** (-ALIBI_MAX_BIAS * h / N_ATT_HEADS)).reshape(N_GROUPS, HEADS_PER_GROUP)


def _moe_block(x, x_bf, logits, w_up, b_up, w_down, b_down, layer):
    row_of, src_tok, gate_row, tile_expert, n_used = _route(logits)
    x_rows = jnp.take(x_bf, src_tok, axis=0)
    y_rows = _moe_experts(x_rows, gate_row, tile_expert, n_used, w_up, b_up, w_down, b_down, layer)
    return jnp.take(y_rows, row_of, axis=0).reshape(x.shape[0], TOP_K, D_MODEL).sum(axis=1)


def kernel(x_prompt, x_sample, state_dn_S, state_dn_conv, cache_kv_w128, cache_kv_w512, cache_kv_w2048,
           w_dn_in, w_dn_conv, dn_a_log, dn_dt_bias, dn_norm_w, w_dn_out, w_kv_shared, w_att_q, w_att_out,
           ln_g, ln_b, w_router, b_router, w_up, b_up, w_down, b_down):
    bp, sp, _ = x_prompt.shape
    bs, ts, _ = x_sample.shape
    n_p, n_s = bp * sp, bs * ts
    kv_caches = (cache_kv_w128, cache_kv_w512, cache_kv_w2048)
    x = jnp.concatenate([x_prompt.reshape(n_p, D_MODEL), x_sample.reshape(n_s, D_MODEL)], axis=0)

    w_in = w_dn_in.reshape(D_MODEL, DN_IN_DIM)
    x_bf = x.astype(BF16)
    n_main = DN_CONV_DIM + DN_VAL_DIM
    proj = _dense(x_bf, w_in, n_cols=n_main)
    tail = _dense(x_bf, w_in, col_block_offset=n_main // LANES, n_cols=LANES, tn=LANES)[:, :2 * DN_V_HEADS]
    gates = _dn_gate_layout(tail)
    head_params = _dn_head_params(dn_a_log[0], dn_dt_bias[0])
    pad_s = DN_CHUNK - ts
    proj_s = proj[n_p:].reshape(bs, ts, n_main)
    o_p, s_p = _deltanet(proj, gates, jnp.zeros((bp, SUBLANES, DN_CONV_DIM), F32),
                         jnp.zeros((bp, DN_V_HEADS, DN_HEAD_DIM, DN_HEAD_DIM), F32), w_dn_conv[0], head_params,
                         dn_norm_w[0], batch=bp, n_chunks=sp // DN_CHUNK, n_valid=DN_CHUNK)
    o_s, s_s = _deltanet(jnp.pad(proj_s, ((0, 0), (0, pad_s), (0, 0))).reshape(bs * DN_CHUNK, n_main),
                         jnp.pad(gates[n_p:].reshape(bs, ts, -1), ((0, 0), (0, pad_s), (0, 0))).reshape(bs * DN_CHUNK, -1),
                         jnp.pad(state_dn_conv[0], ((0, 0), (SUBLANES - (DN_CONV_W - 1), 0), (0, 0))),
                         state_dn_S[0], w_dn_conv[0], head_params, dn_norm_w[0], batch=bs, n_chunks=1, n_valid=ts)
    cbuf_p = proj[:n_p].reshape(bp, sp, n_main)[:, sp - (DN_CONV_W - 1):, :DN_CONV_DIM]
    cbuf_s = proj_s[:, ts - (DN_CONV_W - 1):, :DN_CONV_DIM]
    o = jnp.concatenate([o_p, o_s.reshape(bs, DN_CHUNK, DN_VAL_DIM)[:, :ts].reshape(n_s, DN_VAL_DIM)], axis=0)
    mix = _dense(o, w_dn_out.reshape(DN_VAL_DIM, D_MODEL))
    x, x_bf, logits = _postnorm(x, mix, ln_g[0, 0], ln_b[0, 0], w_router[0], b_router[0])
    ffn = _moe_block(x, x_bf, logits, w_up, b_up, w_down, b_down, 0)
    x, x_bf = _postnorm(x, ffn, ln_g[0, 1], ln_b[0, 1])

    kv = _dense(x_bf, w_kv_shared)
    kv_p = kv[:n_p].reshape(bp, sp, 2, N_GROUPS, HEADS_PER_GROUP, ATT_HEAD_DIM)
    kv_s = kv[n_p:].reshape(bs, ts, 2, N_GROUPS, HEADS_PER_GROUP, ATT_HEAD_DIM)
    new_kv_p = [kv_p[:, sp - min(GROUP_WINDOWS[g], sp):, :, g] for g in range(N_GROUPS)]
    new_kv_s = [jnp.concatenate([kv_caches[g][:, ts:], kv_s[:, :, :, g]], axis=1) for g in range(N_GROUPS)]

    slopes = _alibi_slopes()
    q = _dense(x_bf, w_att_q.reshape(D_MODEL, N_ATT_HEADS * ATT_HEAD_DIM))
    q_s = q[n_p:].reshape(bs, ts, -1)
    kv_s_rows = kv[n_p:].reshape(bs, ts, -1)
    outs_p, lses_p, outs_s, lses_s = [], [], [], []
    for g in range(N_GROUPS):
        o_g, lse_g = _att_prompt(q, kv, slopes, group=g, batch=bp, seq=sp)
        outs_p.append(o_g)
        lses_p.append(lse_g)
        o_g, lse_g = _att_sample(q_s, kv_s_rows, kv_caches[g], slopes, group=g)
        outs_s.append(o_g)
        lses_s.append(lse_g)
    att = jnp.concatenate([_att_merge(outs_p, lses_p), _att_merge(outs_s, lses_s)], axis=0)
    mix = _dense(att, w_att_out.reshape(ATT_SLOT_DIM, D_MODEL))
    x, x_bf, logits = _postnorm(x, mix, ln_g[1, 0], ln_b[1, 0], w_router[1], b_router[1])
    ffn = _moe_block(x, x_bf, logits, w_up, b_up, w_down, b_down, 1)
    x, _ = _postnorm(x, ffn, ln_g[1, 1], ln_b[1, 1])

    y_prompt = x[:n_p].reshape(bp, sp, D_MODEL)
    y_sample = x[n_p:].reshape(bs, ts, D_MODEL)
    return (y_prompt, y_sample, s_p[None], cbuf_p[None], new_kv_p[0], new_kv_p[1], new_kv_p[2],
            s_s[None], cbuf_s[None], new_kv_s[0], new_kv_s[1], new_kv_s[2])
```

```python
import functools
import math

import jax
import jax.numpy as jnp
from jax import lax
from jax.experimental import pallas as pl
from jax.experimental.pallas import tpu as pltpu

D_MODEL = 4096
DEPTH = 2
N_A_LAYERS = DEPTH // 2

DN_QK_HEADS = 16
DN_V_HEADS = 32
DN_HEAD_DIM = 128
DN_KEY_DIM = DN_QK_HEADS * DN_HEAD_DIM
DN_VAL_DIM = DN_V_HEADS * DN_HEAD_DIM
DN_CONV_W = 4
DN_CONV_DIM = 2 * DN_KEY_DIM + DN_VAL_DIM
DN_IN_DIM = DN_CONV_DIM + DN_VAL_DIM + 2 * DN_V_HEADS
DN_CHUNK = 64

GROUP_WINDOWS = (128, 512, 2048)
GROUP_DILATIONS = (1, 4, 16)
N_GROUPS = 3
HEADS_PER_GROUP = 16
ATT_HEAD_DIM = 128
N_ATT_HEADS = N_GROUPS * HEADS_PER_GROUP
ATT_SLOT_DIM = HEADS_PER_GROUP * ATT_HEAD_DIM
ALIBI_MAX_BIAS = 8.0

N_EXPERTS = 32
TOP_K = 4
D_EXPERT = D_MODEL // 2
SWIGLU_LIMIT = 7.0
SWIGLU_ALPHA = 1.702

LN_EPS = 1e-5
NORM_EPS = 1e-6
DEEP_ALPHA = (2.0 * DEPTH) ** 0.25

LANES = 128
SUBLANES = 8
MXU_DIM = 256
VMEM_LIMIT_BYTES = 56 * 1024 * 1024

DENSE_TM = 688
DENSE_TN = 512
MOE_TM = 256
MOE_UP_TN = 1024
MOE_DOWN_TN = 1024

BF16 = jnp.bfloat16
F32 = jnp.float32


def _dense_kernel(x_ref, w_ref, o_ref, wbf_ref):
    @pl.when(pl.program_id(1) == 0)
    def _():
        wbf_ref[...] = w_ref[...].astype(BF16)

    o_ref[...] = jnp.dot(x_ref[...], wbf_ref[...], preferred_element_type=F32).astype(o_ref.dtype)


def _dense(x, w, *, col_block_offset=0, n_cols=None, tn=DENSE_TN, out_dtype=F32):
    m, k = x.shape
    n_cols = w.shape[1] if n_cols is None else n_cols
    tm = DENSE_TM if m % DENSE_TM == 0 else m
    assert m % tm == 0 and n_cols % tn == 0
    return pl.pallas_call(
        _dense_kernel,
        grid=(n_cols // tn, m // tm),
        in_specs=[pl.BlockSpec((tm, k), lambda j, i: (i, 0)),
                  pl.BlockSpec((k, tn), lambda j, i: (0, j + col_block_offset))],
        out_specs=pl.BlockSpec((tm, tn), lambda j, i: (i, j)),
        out_shape=jax.ShapeDtypeStruct((m, n_cols), out_dtype),
        scratch_shapes=[pltpu.VMEM((k, tn), BF16)],
        compiler_params=pltpu.CompilerParams(dimension_semantics=("arbitrary", "arbitrary"),
                                             vmem_limit_bytes=VMEM_LIMIT_BYTES),
        name="dense_proj",
    )(x, w)


def _moe_rows(n_tokens):
    n_assign = n_tokens * TOP_K
    n_tiles = -(-(n_assign + N_EXPERTS * (MOE_TM - 1)) // MOE_TM)
    return n_tiles, n_tiles * MOE_TM


def _route(logits):
    n_tokens = logits.shape[0]
    n_tiles, n_rows = _moe_rows(n_tokens)
    top_vals, top_idx = lax.top_k(logits, TOP_K)
    gates = jax.nn.softmax(top_vals, axis=-1).reshape(-1)
    flat_e = top_idx.reshape(-1).astype(jnp.int32)
    onehot = (flat_e[:, None] == jnp.arange(N_EXPERTS, dtype=jnp.int32)[None, :]).astype(jnp.int32)
    csum = jnp.cumsum(onehot, axis=0)
    rank = jnp.take_along_axis(csum, flat_e[:, None], axis=1)[:, 0] - 1
    counts = csum[-1]
    padded = ((counts + MOE_TM - 1) // MOE_TM) * MOE_TM
    pend = jnp.cumsum(padded)
    pstart = pend - padded
    row_of = pstart[flat_e] + rank
    src_tok = jnp.zeros((n_rows,), jnp.int32).at[row_of].set(jnp.arange(flat_e.shape[0], dtype=jnp.int32) // TOP_K)
    gate_row = jnp.zeros((n_rows,), F32).at[row_of].set(gates)
    n_used = (pend[-1] // MOE_TM).astype(jnp.int32)
    tile_start = jnp.minimum(jnp.arange(n_tiles, dtype=jnp.int32), n_used - 1) * MOE_TM
    tile_expert = jnp.minimum(jnp.searchsorted(pend, tile_start, side="right"), N_EXPERTS - 1).astype(jnp.int32)
    return row_of, src_tok, gate_row, tile_expert, n_used.reshape(1)


def _weights_changed(te_ref, i):
    return (i == 0) | (te_ref[i] != te_ref[jnp.maximum(i - 1, 0)])


def _moe_up_kernel(te_ref, nu_ref, x_ref, w_ref, b_ref, o_ref, wbf_ref):
    i = pl.program_id(1)

    @pl.when(_weights_changed(te_ref, i))
    def _():
        wbf_ref[...] = w_ref[0, 0].astype(BF16)

    @pl.when(i < nu_ref[0])
    def _():
        h = jnp.dot(x_ref[...], wbf_ref[...], preferred_element_type=F32) + b_ref[0]
        glu = jnp.minimum(h, SWIGLU_LIMIT)
        glu = glu * jax.nn.sigmoid(SWIGLU_ALPHA * glu)
        lin = jnp.clip(h, -SWIGLU_LIMIT, SWIGLU_LIMIT) + 1.0
        tn = h.shape[1]
        prod = glu * pltpu.roll(lin, tn - 1, 1)
        lane = lax.broadcasted_iota(jnp.int32, prod.shape, 1)
        prod = jnp.where(lane % 2 == 0, prod, 0.0).astype(BF16)
        r = lax.broadcasted_iota(jnp.int32, (MXU_DIM, MXU_DIM // 2), 0)
        c = lax.broadcasted_iota(jnp.int32, (MXU_DIM, MXU_DIM // 2), 1)
        sel = (r == 2 * c).astype(BF16)
        for q in range(tn // MXU_DIM):
            part = jnp.dot(prod[:, q * MXU_DIM:(q + 1) * MXU_DIM], sel, preferred_element_type=F32)
            o_ref[:, q * (MXU_DIM // 2):(q + 1) * (MXU_DIM // 2)] = part.astype(o_ref.dtype)


def _moe_down_kernel(te_ref, nu_ref, h_ref, w_ref, b_ref, g_ref, o_ref, wbf_ref):
    i = pl.program_id(1)

    @pl.when(_weights_changed(te_ref, i))
    def _():
        wbf_ref[...] = w_ref[0, 0].astype(BF16)

    @pl.when(i < nu_ref[0])
    def _():
        y = jnp.dot(h_ref[...], wbf_ref[...], preferred_element_type=F32) + b_ref[0]
        o_ref[...] = y * g_ref[...]


def _moe_experts(x_rows, gate_row, tile_expert, n_used, w_up, b_up, w_down, b_down, layer):
    n_rows = x_rows.shape[0]
    n_tiles = n_rows // MOE_TM
    tm = MOE_TM

    def row_tile(j, i, te, nu):
        return (jnp.minimum(i, nu[0] - 1), 0)

    tn = MOE_UP_TN
    hact = pl.pallas_call(
        _moe_up_kernel,
        grid_spec=pltpu.PrefetchScalarGridSpec(
            num_scalar_prefetch=2,
            grid=(2 * D_EXPERT // tn, n_tiles),
            in_specs=[pl.BlockSpec((tm, D_MODEL), row_tile),
                      pl.BlockSpec((1, 1, D_MODEL, tn), lambda j, i, te, nu: (layer, te[i], 0, j)),
                      pl.BlockSpec((1, 1, tn), lambda j, i, te, nu: (layer * N_EXPERTS + te[i], 0, j))],
            out_specs=pl.BlockSpec((tm, tn // 2), lambda j, i, te, nu: (jnp.minimum(i, nu[0] - 1), j)),
            scratch_shapes=[pltpu.VMEM((D_MODEL, tn), BF16)]),
        out_shape=jax.ShapeDtypeStruct((n_rows, D_EXPERT), BF16),
        compiler_params=pltpu.CompilerParams(dimension_semantics=("arbitrary", "arbitrary"),
                                             vmem_limit_bytes=VMEM_LIMIT_BYTES),
        name="moe_up_swiglu",
    )(tile_expert, n_used, x_rows, w_up, b_up.reshape(DEPTH * N_EXPERTS, 1, 2 * D_EXPERT))

    tn = MOE_DOWN_TN
    return pl.pallas_call(
        _moe_down_kernel,
        grid_spec=pltpu.PrefetchScalarGridSpec(
            num_scalar_prefetch=2,
            grid=(D_MODEL // tn, n_tiles),
            in_specs=[pl.BlockSpec((tm, D_EXPERT), row_tile),
                      pl.BlockSpec((1, 1, D_EXPERT, tn), lambda j, i, te, nu: (layer, te[i], 0, j)),
                      pl.BlockSpec((1, 1, tn), lambda j, i, te, nu: (layer * N_EXPERTS + te[i], 0, j)),
                      pl.BlockSpec((tm, 1), row_tile)],
            out_specs=pl.BlockSpec((tm, tn), lambda j, i, te, nu: (jnp.minimum(i, nu[0] - 1), j)),
            scratch_shapes=[pltpu.VMEM((D_EXPERT, tn), BF16)]),
        out_shape=jax.ShapeDtypeStruct((n_rows, D_MODEL), F32),
        compiler_params=pltpu.CompilerParams(dimension_semantics=("arbitrary", "arbitrary"),
                                             vmem_limit_bytes=VMEM_LIMIT_BYTES),
        name="moe_down",
    )(tile_expert, n_used, hact, w_down, b_down.reshape(DEPTH * N_EXPERTS, 1, D_MODEL), gate_row.reshape(n_rows, 1))


DN_VH_STEP = 8
DN_QKH_STEP = DN_VH_STEP // (DN_V_HEADS // DN_QK_HEADS)
DN_GROUPS = DN_V_HEADS // DN_VH_STEP
DN_BASE_BLOCK = 16


def _dot(a, b):
    return jnp.dot(a, b, preferred_element_type=F32)


def _dot_nt(a, b):
    return lax.dot_general(a, b, (((1,), (1,)), ((), ())), preferred_element_type=F32)


def _split(a):
    hi = a.astype(BF16)
    return hi, (a - hi.astype(F32)).astype(BF16)


def _dot3(a, b, dot=_dot):
    return dot(a[0], b[0]) + (dot(a[0], b[1]) + dot(a[1], b[0]))


def _unit_lower_inverses(lows, ii, jj):
    eye = (ii == jj).astype(F32)
    blk16 = (ii // DN_BASE_BLOCK) == (jj // DN_BASE_BLOCK)
    blk32 = (ii // (2 * DN_BASE_BLOCK)) == (jj // (2 * DN_BASE_BLOCK))
    ps = [jnp.where(blk16, -low, 0.0) for low in lows]
    ts = [eye + p for p in ps]
    for _ in range(3):
        pss = [_split(p) for p in ps]
        ps = [_dot3(s, s) for s in pss]
        ts = [t + _dot3(_split(t), _split(p)) for t, p in zip(ts, ps)]
    for sel in (blk32 & ~blk16, ~blk32):
        tss = [_split(t) for t in ts]
        us = [_dot3(_split(jnp.where(sel, low, 0.0)), s) for low, s in zip(lows, tss)]
        ts = [t - _dot3(s, _split(u)) for t, s, u in zip(ts, tss, us)]
    return ts


def _dn_kernel(q_ref, k_ref, v_ref, z_ref, gate_ref, cq_ref, ck_ref, cv_ref, s0_ref, wq_ref, wk_ref, wv_ref,
               hp_ref, nw_ref, o_ref, s_ref, pq_ref, pk_ref, pv_ref, *, n_valid):
    c = DN_CHUNK
    hd = DN_HEAD_DIM

    @pl.when(pl.program_id(2) == 0)
    def _():
        pq_ref[c - SUBLANES:c, :] = cq_ref[0]
        pk_ref[c - SUBLANES:c, :] = ck_ref[0]
        pv_ref[c - SUBLANES:c, :] = cv_ref[0]
        s_ref[...] = s0_ref[...]

    row = lax.broadcasted_iota(jnp.int32, (c, 1), 0)
    ii = lax.broadcasted_iota(jnp.int32, (c, c), 0)
    jj = lax.broadcasted_iota(jnp.int32, (c, c), 1)

    def conv_silu(u_ref, prev_ref, w_ref):
        u = u_ref[...]
        prev = prev_ref[...]
        y = u * w_ref[DN_CONV_W - 1:DN_CONV_W, :]
        for s in range(1, DN_CONV_W):
            shifted = jnp.where(row >= s, pltpu.roll(u, s, 0), pltpu.roll(prev, s, 0))
            y = y + shifted * w_ref[DN_CONV_W - 1 - s:DN_CONV_W - s, :]
        prev_ref[...] = u
        return y * jax.nn.sigmoid(y)

    q_all = conv_silu(q_ref, pq_ref, wq_ref)
    k_all = conv_silu(k_ref, pk_ref, wk_ref)
    v_all = conv_silu(v_ref, pv_ref, wv_ref)

    gt = gate_ref[...]
    beta_all = jax.nn.sigmoid(gt)
    pre = gt + hp_ref[0, 0:1, :]
    softplus = jnp.maximum(pre, 0.0) + jnp.log1p(jnp.exp(-jnp.abs(pre)))
    g_all = -jnp.exp(hp_ref[0, 1:2, :]) * softplus
    if n_valid < c:
        live = row < n_valid
        beta_all = jnp.where(live, beta_all, 0.0)
        g_all = jnp.where(live, g_all, 0.0)
        k_all = jnp.where(live, k_all, 0.0)
        v_all = jnp.where(live, v_all, 0.0)
    gc = g_all
    s = 1
    while s < c:
        gc = gc + jnp.where(row >= s, pltpu.roll(gc, s, 0), 0.0)
        s *= 2
    gc_t = gc.T

    def l2n(t):
        return t * lax.rsqrt(jnp.sum(t * t, axis=-1, keepdims=True) + NORM_EPS)

    heads = range(DN_VH_STEP)
    rep = DN_VH_STEP // DN_QKH_STEP
    q_n = [l2n(q_all[:, j * hd:(j + 1) * hd]) * hd ** -0.5 for j in range(DN_QKH_STEP)]
    k_n = [l2n(k_all[:, j * hd:(j + 1) * hd]) for j in range(DN_QKH_STEP)]
    kk_qk = [_dot3(_split(jnp.concatenate([k, q], axis=0)), _split(k), _dot_nt) for k, q in zip(k_n, q_n)]
    g_col = [gc[:, SUBLANES + h:SUBLANES + h + 1] for h in heads]
    beta = [beta_all[:, h:h + 1] for h in heads]
    decay = [jnp.exp(jnp.minimum(g_col[h] - gc_t[SUBLANES + h:SUBLANES + h + 1, :], 0.0)) for h in heads]
    lows = [jnp.where(ii > jj, beta[h] * kk_qk[h // rep][:c] * decay[h], 0.0) for h in heads]
    t_inv = _unit_lower_inverses(lows, ii, jj)
    e_col = [jnp.exp(g) for g in g_col]
    rhs = [jnp.concatenate([v_all[:, h * hd:(h + 1) * hd] * beta[h], k_n[h // rep] * (beta[h] * e_col[h])], axis=1)
           for h in heads]
    w = [_dot3(_split(t_inv[h]), _split(rhs[h])) for h in heads]
    states = [s_ref[0, h] for h in heads]
    ks_qs = [_dot3(_split(jnp.concatenate([w[h][:, hd:], q_n[h // rep] * e_col[h]], axis=0)), _split(states[h]))
             for h in heads]
    v_new = [w[h][:, :hd] - ks_qs[h][:c] for h in heads]
    g_last = [g[c - 1:c, :] for g in g_col]
    k_tail = [k_n[h // rep] * jnp.exp(g_last[h] - g_col[h]) for h in heads]
    for h in heads:
        s_ref[0, h] = states[h] * jnp.exp(g_last[h]) + _dot3(_split(k_tail[h].T), _split(v_new[h]))
    attn = [jnp.where(ii >= jj, kk_qk[h // rep][c:] * decay[h], 0.0) for h in heads]
    outs = [ks_qs[h][c:] + _dot(attn[h].astype(BF16), v_new[h].astype(BF16)) for h in heads]
    for h in heads:
        o = outs[h]
        o = o * lax.rsqrt(jnp.mean(o * o, axis=-1, keepdims=True) + NORM_EPS) * nw_ref[...]
        z_h = z_ref[:, h * hd:(h + 1) * hd]
        o_ref[:, h * hd:(h + 1) * hd] = (o * (z_h * jax.nn.sigmoid(z_h))).astype(o_ref.dtype)


def _deltanet(proj, gates, conv_init, s0, w_conv, head_params, norm_w, *, batch, n_chunks, n_valid):
    nc = n_chunks
    qw = DN_QKH_STEP * DN_HEAD_DIM
    vw = DN_VH_STEP * DN_HEAD_DIM
    k_off = DN_KEY_DIM // qw
    v_off = 2 * DN_KEY_DIM // vw
    z_off = DN_CONV_DIM // vw
    chunk = lambda width, off: pl.BlockSpec((DN_CHUNK, width), lambda bi, g, ci: (bi * nc + ci, off + g))
    init = lambda width, off: pl.BlockSpec((1, SUBLANES, width), lambda bi, g, ci: (bi, 0, off + g))
    wspec = lambda width, off: pl.BlockSpec((DN_CONV_W, width), lambda bi, g, ci: (0, off + g))
    state = pl.BlockSpec((1, DN_VH_STEP, DN_HEAD_DIM, DN_HEAD_DIM), lambda bi, g, ci: (bi, g, 0, 0))
    return pl.pallas_call(
        functools.partial(_dn_kernel, n_valid=n_valid),
        grid=(batch, DN_GROUPS, nc),
        in_specs=[chunk(qw, 0), chunk(qw, k_off), chunk(vw, v_off), chunk(vw, z_off), chunk(LANES, 0),
                  init(qw, 0), init(qw, k_off), init(vw, v_off), state,
                  wspec(qw, 0), wspec(qw, k_off), wspec(vw, v_off),
                  pl.BlockSpec((1, SUBLANES, LANES), lambda bi, g, ci: (g, 0, 0)),
                  pl.BlockSpec((1, DN_HEAD_DIM), lambda bi, g, ci: (0, 0))],
        out_specs=[chunk(vw, 0), state],
        out_shape=[jax.ShapeDtypeStruct((batch * nc * DN_CHUNK, DN_VAL_DIM), BF16),
                   jax.ShapeDtypeStruct((batch, DN_V_HEADS, DN_HEAD_DIM, DN_HEAD_DIM), F32)],
        scratch_shapes=[pltpu.VMEM((DN_CHUNK, qw), F32), pltpu.VMEM((DN_CHUNK, qw), F32),
                        pltpu.VMEM((DN_CHUNK, vw), F32)],
        compiler_params=pltpu.CompilerParams(dimension_semantics=("arbitrary", "arbitrary", "arbitrary"),
                                             vmem_limit_bytes=VMEM_LIMIT_BYTES),
        name="deltanet_chunk",
    )(proj, proj, proj, proj, gates, conv_init, conv_init, conv_init, s0, w_conv, w_conv, w_conv,
      head_params, norm_w.reshape(1, DN_HEAD_DIM))


def _dn_gate_layout(tail):
    lead = tail.shape[:-1]
    t = tail.reshape(lead + (2, DN_GROUPS, DN_VH_STEP))
    t = jnp.swapaxes(t, -3, -2).reshape(lead + (DN_GROUPS, 2 * DN_VH_STEP))
    t = jnp.pad(t, [(0, 0)] * (len(lead) + 1) + [(0, LANES - 2 * DN_VH_STEP)])
    return t.reshape(lead + (DN_GROUPS * LANES,))


def _dn_head_params(a_log, dt_bias):
    def lay(p):
        p = p.reshape(DN_GROUPS, 1, DN_VH_STEP)
        return jnp.pad(p, ((0, 0), (0, 0), (DN_VH_STEP, LANES - 2 * DN_VH_STEP)))
    rows = jnp.concatenate([lay(dt_bias), lay(a_log)], axis=1)
    return jnp.pad(rows, ((0, 0), (0, SUBLANES - 2), (0, 0)))


ATT_BLK = 128
ATT_MASKED = -1e30


ATT_P_HEADS = (16, 8, 4)
ATT_CLASS_UNROLL = 16


def _att_prompt_kernel(slope_ref, q_ref, *refs, group, dilation, heads, has_prev):
    n_in = 4 if has_prev else 2
    kv_refs, (o_ref, lse_ref), stage = refs[:n_in], refs[n_in:n_in + 2], refs[n_in + 2:]
    hd = ATT_HEAD_DIM
    blk = ATT_BLK
    n_keys = 2 * blk if has_prev else blk
    qi = lax.broadcasted_iota(jnp.int32, (blk, n_keys), 0)
    kj = lax.broadcasted_iota(jnp.int32, (blk, n_keys), 1)
    dist = qi + (n_keys - blk) - kj
    valid = (dist >= 0) & (dist <= blk)
    if has_prev:
        valid = valid & ((kj >= blk) | (pl.program_id(1) > 0))
    dist_f = (dist * dilation).astype(F32)
    lane = lax.broadcasted_iota(jnp.int32, (blk, LANES), 1)
    head0 = pl.program_id(2) * heads

    @pl.when(pl.program_id(2) == 0)
    def _():
        lse_ref[...] = jnp.zeros(lse_ref.shape, F32)

    for h in range(heads):
        hs = slice(h * hd, (h + 1) * hd)
        slope = slope_ref[group, head0 + h]
        if dilation > 1:
            q_st, o_st = stage[0], stage[1]
            q_st[...] = q_ref[:, hs]
            for src, dst in zip(kv_refs, stage[2:]):
                dst[...] = src[:, hs]
            kv_src = stage[2:]
        else:
            kv_src = kv_refs

        def one_class(r, carry):
            if dilation > 1:
                rows = pl.ds(r, blk, stride=dilation)
                q = q_st[rows, :]
                parts = [ref[rows, :] for ref in kv_src]
            else:
                rows = pl.ds(0, blk)
                q = q_ref[:, hs]
                parts = [ref[:, hs] for ref in kv_src]
            if has_prev:
                k = jnp.concatenate([parts[1], parts[0]], axis=0)
                v = jnp.concatenate([parts[3], parts[2]], axis=0)
            else:
                k, v = parts
            sc = _dot_nt(q.astype(BF16), k.astype(BF16)) * hd ** -0.5 - slope * dist_f
            sc = jnp.where(valid, sc, ATT_MASKED)
            m = jnp.max(sc, axis=-1, keepdims=True)
            p = jnp.exp(sc - m)
            l = jnp.sum(p, axis=-1, keepdims=True)
            o = _dot(p.astype(BF16), v.astype(BF16)) / l
            if dilation > 1:
                o_st[rows, :] = o
            else:
                o_ref[:, hs] = o
            lse_ref[rows, :] = jnp.where(lane == head0 + h, m + jnp.log(l), lse_ref[rows, :])
            return carry

        if dilation > 1:
            lax.fori_loop(0, dilation, one_class, 0, unroll=ATT_CLASS_UNROLL)
            o_ref[:, hs] = o_st[...]
        else:
            one_class(0, 0)


def _att_prompt(q, kv, slopes, *, group, batch, seq):
    d = GROUP_DILATIONS[group]
    heads = ATT_P_HEADS[group]
    span = d * ATT_BLK
    assert GROUP_WINDOWS[group] // d == ATT_BLK and seq % span == 0 and HEADS_PER_GROUP % heads == 0
    ns = seq // span
    has_prev = ns > 1
    w = heads * ATT_HEAD_DIM
    n_hc = ATT_SLOT_DIM // w
    n_kv = kv.shape[1] // w
    cur = lambda col: pl.BlockSpec((span, w), lambda b, n, hc: (b * ns + n, col + hc))
    prev = lambda col: pl.BlockSpec((span, w), lambda b, n, hc: (b * ns + jnp.maximum(n - 1, 0), col + hc))
    k_col, v_col = group * n_hc, n_kv // 2 + group * n_hc
    in_specs = [pl.BlockSpec(memory_space=pltpu.SMEM), cur(group * n_hc)]
    in_specs += [cur(k_col), prev(k_col), cur(v_col), prev(v_col)] if has_prev else [cur(k_col), cur(v_col)]
    return pl.pallas_call(
        functools.partial(_att_prompt_kernel, group=group, dilation=d, heads=heads, has_prev=has_prev),
        grid=(batch, ns, n_hc),
        in_specs=in_specs,
        out_specs=[cur(0), pl.BlockSpec((span, LANES), lambda b, n, hc: (b * ns + n, 0))],
        out_shape=[jax.ShapeDtypeStruct((batch * seq, ATT_SLOT_DIM), F32),
                   jax.ShapeDtypeStruct((batch * seq, LANES), F32)],
        scratch_shapes=[pltpu.VMEM((span, ATT_HEAD_DIM), F32)] * ((4 + 2 * has_prev) if d > 1 else 0),
        compiler_params=pltpu.CompilerParams(dimension_semantics=("arbitrary", "arbitrary", "arbitrary"),
                                             vmem_limit_bytes=VMEM_LIMIT_BYTES),
        name="dilated_attention_prompt",
    )(slopes, q, *([kv] * (4 if has_prev else 2)))


ATT_S_HEADS = 4


def _att_sample_kernel(slope_ref, q_ref, kc_ref, vc_ref, kn_ref, vn_ref, o_ref, lse_ref, *, group, dilation):
    hd = ATT_HEAD_DIM
    t_new = q_ref.shape[1]
    l_buf = kc_ref.shape[1]
    window = GROUP_WINDOWS[group]
    tq = lax.broadcasted_iota(jnp.int32, (t_new, l_buf), 0)
    ic = lax.broadcasted_iota(jnp.int32, (t_new, l_buf), 1)
    dist_c = l_buf + tq - ic
    ok_c = (((ic - tq) & (dilation - 1)) == 0) & (dist_c <= window)
    tn = lax.broadcasted_iota(jnp.int32, (t_new, t_new), 0)
    un = lax.broadcasted_iota(jnp.int32, (t_new, t_new), 1)
    dist_n = tn - un
    ok_n = (dist_n >= 0) & ((dist_n & (dilation - 1)) == 0) & (dist_n <= window)
    lane = lax.broadcasted_iota(jnp.int32, (t_new, LANES), 1)
    lse_all = jnp.zeros((t_new, LANES), F32)
    for h in range(ATT_S_HEADS):
        hs = slice(h * hd, (h + 1) * hd)
        slope = slope_ref[group, pl.program_id(1) * ATT_S_HEADS + h]
        q = q_ref[0, :, hs].astype(BF16)
        sc_c = _dot_nt(q, kc_ref[0, :, hs].astype(BF16)) * hd ** -0.5 - slope * dist_c.astype(F32)
        sc_n = _dot_nt(q, kn_ref[0, :, hs].astype(BF16)) * hd ** -0.5 - slope * dist_n.astype(F32)
        sc_c = jnp.where(ok_c, sc_c, ATT_MASKED)
        sc_n = jnp.where(ok_n, sc_n, ATT_MASKED)
        m = jnp.maximum(jnp.max(sc_c, axis=-1, keepdims=True), jnp.max(sc_n, axis=-1, keepdims=True))
        p_c = jnp.exp(sc_c - m)
        p_n = jnp.exp(sc_n - m)
        l = jnp.sum(p_c, axis=-1, keepdims=True) + jnp.sum(p_n, axis=-1, keepdims=True)
        acc = (_dot(p_c.astype(BF16), vc_ref[0, :, hs].astype(BF16))
               + _dot(p_n.astype(BF16), vn_ref[0, :, hs].astype(BF16)))
        o_ref[0, :, hs] = acc / l
        lse_all = jnp.where(lane == h, m + jnp.log(l), lse_all)
    lse_ref[0, 0] = lse_all


def _att_sample(q, kv_new, cache, slopes, *, group):
    b, t, _ = q.shape
    l_buf = cache.shape[1]
    w = ATT_S_HEADS * ATT_HEAD_DIM
    per_group = ATT_SLOT_DIM // w
    n_q = q.shape[2] // w
    cache2 = cache.reshape(b, l_buf, 2 * ATT_SLOT_DIM)
    new = lambda col: pl.BlockSpec((1, t, w), lambda bi, hc: (bi, 0, col + hc))
    buf = lambda col: pl.BlockSpec((1, l_buf, w), lambda bi, hc: (bi, 0, col + hc))
    o, lse = pl.pallas_call(
        functools.partial(_att_sample_kernel, group=group, dilation=GROUP_DILATIONS[group]),
        grid=(b, per_group),
        in_specs=[pl.BlockSpec(memory_space=pltpu.SMEM), new(group * per_group), buf(0), buf(per_group),
                  new(group * per_group), new(n_q + group * per_group)],
        out_specs=[new(0), pl.BlockSpec((1, 1, t, LANES), lambda bi, hc: (bi, hc, 0, 0))],
        out_shape=[jax.ShapeDtypeStruct((b, t, ATT_SLOT_DIM), F32),
                   jax.ShapeDtypeStruct((b, per_group, t, LANES), F32)],
        compiler_params=pltpu.CompilerParams(dimension_semantics=("arbitrary", "arbitrary"),
                                             vmem_limit_bytes=VMEM_LIMIT_BYTES),
        name="dilated_attention_sample",
    )(slopes, q, cache2, cache2, kv_new, kv_new)
    lse = lse[..., :ATT_S_HEADS].transpose(0, 2, 1, 3).reshape(b * t, HEADS_PER_GROUP)
    return o.reshape(b * t, ATT_SLOT_DIM), jnp.pad(lse, ((0, 0), (0, LANES - HEADS_PER_GROUP)))


def _att_merge_kernel(o0_ref, o1_ref, o2_ref, l0_ref, l1_ref, l2_ref, out_ref):
    hd = ATT_HEAD_DIM
    lses = [l0_ref[...], l1_ref[...], l2_ref[...]]
    m = jnp.maximum(jnp.maximum(lses[0], lses[1]), lses[2])
    zs = [jnp.exp(l - m) for l in lses]
    inv = 1.0 / (zs[0] + zs[1] + zs[2])
    for h in range(HEADS_PER_GROUP):
        hs = slice(h * hd, (h + 1) * hd)
        acc = (zs[0][:, h:h + 1] * inv[:, h:h + 1]) * o0_ref[:, hs]
        acc = acc + (zs[1][:, h:h + 1] * inv[:, h:h + 1]) * o1_ref[:, hs]
        acc = acc + (zs[2][:, h:h + 1] * inv[:, h:h + 1]) * o2_ref[:, hs]
        out_ref[:, hs] = acc.astype(out_ref.dtype)


def _att_merge(outs, lses):
    n = outs[0].shape[0]
    tm = 256 if n % 256 == 0 else n
    ospec = pl.BlockSpec((tm, ATT_SLOT_DIM), lambda i: (i, 0))
    lspec = pl.BlockSpec((tm, LANES), lambda i: (i, 0))
    return pl.pallas_call(
        _att_merge_kernel,
        grid=(n // tm,),
        in_specs=[ospec] * 3 + [lspec] * 3,
        out_specs=ospec,
        out_shape=jax.ShapeDtypeStruct((n, ATT_SLOT_DIM), BF16),
        compiler_params=pltpu.CompilerParams(dimension_semantics=("arbitrary",), vmem_limit_bytes=VMEM_LIMIT_BYTES),
        name="attention_group_merge",
    )(*outs, *lses)


NORM_TM = 192


def _postnorm_kernel(x_ref, f_ref, g_ref, b_ref, *rest, with_router):
    y = DEEP_ALPHA * x_ref[...] + f_ref[...]
    mu = jnp.mean(y, axis=-1, keepdims=True)
    yc = y - mu
    var = jnp.mean(yc * yc, axis=-1, keepdims=True)
    out = yc * lax.rsqrt(var + LN_EPS) * g_ref[...] + b_ref[...]
    if with_router:
        wr_ref, br_ref, o_ref, obf_ref, lg_ref = rest
        lg_ref[...] = jnp.dot(out, wr_ref[...], precision=lax.Precision.HIGHEST,
                              preferred_element_type=F32) + br_ref[...]
    else:
        o_ref, obf_ref = rest
    o_ref[...] = out
    obf_ref[...] = out.astype(BF16)


def _postnorm(x, f, g, b, w_router=None, b_router=None):
    n, d = x.shape
    tm = NORM_TM if n % NORM_TM == 0 else n
    row = pl.BlockSpec((tm, d), lambda i: (i, 0))
    vec = pl.BlockSpec((1, d), lambda i: (0, 0))
    with_router = w_router is not None
    in_specs = [row, row, vec, vec]
    args = [x, f, g.reshape(1, d), b.reshape(1, d)]
    out_specs = [row, row]
    out_shape = [jax.ShapeDtypeStruct((n, d), F32), jax.ShapeDtypeStruct((n, d), BF16)]
    if with_router:
        in_specs += [pl.BlockSpec((d, N_EXPERTS), lambda i: (0, 0)), pl.BlockSpec((1, N_EXPERTS), lambda i: (0, 0))]
        args += [w_router, b_router.reshape(1, N_EXPERTS)]
        out_specs.append(pl.BlockSpec((tm, N_EXPERTS), lambda i: (i, 0)))
        out_shape.append(jax.ShapeDtypeStruct((n, N_EXPERTS), F32))
    return pl.pallas_call(
        functools.partial(_postnorm_kernel, with_router=with_router),
        grid=(n // tm,),
        in_specs=in_specs, out_specs=out_specs, out_shape=out_shape,
        compiler_params=pltpu.CompilerParams(dimension_semantics=("arbitrary",), vmem_limit_bytes=VMEM_LIMIT_BYTES),
        name="postnorm_router" if with_router else "postnorm",
    )(*args)


def _alibi_slopes():
    h = jnp.arange(1, N_ATT_HEADS + 1, dtype=F32)
    return (2.0 ** (-ALIBI_MAX_BIAS * h / N_ATT_HEADS)).reshape(N_GROUPS, HEADS_PER_GROUP)


def _moe_block(x, x_bf, logits, w_up, b_up, w_down, b_down, layer):
    row_of, src_tok, gate_row, tile_expert, n_used = _route(logits)
    x_rows = jnp.take(x_bf, src_tok, axis=0)
    y_rows = _moe_experts(x_rows, gate_row, tile_expert, n_used, w_up, b_up, w_down, b_down, layer)
    return jnp.take(y_rows, row_of, axis=0).reshape(x.shape[0], TOP_K, D_MODEL).sum(axis=1)


def kernel(x_prompt, x_sample, state_dn_S, state_dn_conv, cache_kv_w128, cache_kv_w512, cache_kv_w2048,
           w_dn_in, w_dn_conv, dn_a_log, dn_dt_bias, dn_norm_w, w_dn_out, w_kv_shared, w_att_q, w_att_out,
           ln_g, ln_b, w_router, b_router, w_up, b_up, w_down, b_down):
    bp, sp, _ = x_prompt.shape
    bs, ts, _ = x_sample.shape
    n_p, n_s = bp * sp, bs * ts
    kv_caches = (cache_kv_w128, cache_kv_w512, cache_kv_w2048)
    x = jnp.concatenate([x_prompt.reshape(n_p, D_MODEL), x_sample.reshape(n_s, D_MODEL)], axis=0)

    w_in = w_dn_in.reshape(D_MODEL, DN_IN_DIM)
    x_bf = x.astype(BF16)
    n_main = DN_CONV_DIM + DN_VAL_DIM
    proj = _dense(x_bf, w_in, n_cols=n_main)
    tail = _dense(x_bf, w_in, col_block_offset=n_main // LANES, n_cols=LANES, tn=LANES)[:, :2 * DN_V_HEADS]
    gates = _dn_gate_layout(tail)
    head_params = _dn_head_params(dn_a_log[0], dn_dt_bias[0])
    pad_s = DN_CHUNK - ts
    proj_s = proj[n_p:].reshape(bs, ts, n_main)
    o_p, s_p = _deltanet(proj, gates, jnp.zeros((bp, SUBLANES, DN_CONV_DIM), F32),
                         jnp.zeros((bp, DN_V_HEADS, DN_HEAD_DIM, DN_HEAD_DIM), F32), w_dn_conv[0], head_params,
                         dn_norm_w[0], batch=bp, n_chunks=sp // DN_CHUNK, n_valid=DN_CHUNK)
    o_s, s_s = _deltanet(jnp.pad(proj_s, ((0, 0), (0, pad_s), (0, 0))).reshape(bs * DN_CHUNK, n_main),
                         jnp.pad(gates[n_p:].reshape(bs, ts, -1), ((0, 0), (0, pad_s), (0, 0))).reshape(bs * DN_CHUNK, -1),
                         jnp.pad(state_dn_conv[0], ((0, 0), (SUBLANES - (DN_CONV_W - 1), 0), (0, 0))),
                         state_dn_S[0], w_dn_conv[0], head_params, dn_norm_w[0], batch=bs, n_chunks=1, n_valid=ts)
    cbuf_p = proj[:n_p].reshape(bp, sp, n_main)[:, sp - (DN_CONV_W - 1):, :DN_CONV_DIM]
    cbuf_s = proj_s[:, ts - (DN_CONV_W - 1):, :DN_CONV_DIM]
    o = jnp.concatenate([o_p, o_s.reshape(bs, DN_CHUNK, DN_VAL_DIM)[:, :ts].reshape(n_s, DN_VAL_DIM)], axis=0)
    mix = _dense(o, w_dn_out.reshape(DN_VAL_DIM, D_MODEL))
    x, x_bf, logits = _postnorm(x, mix, ln_g[0, 0], ln_b[0, 0], w_router[0], b_router[0])
    ffn = _moe_block(x, x_bf, logits, w_up, b_up, w_down, b_down, 0)
    x, x_bf = _postnorm(x, ffn, ln_g[0, 1], ln_b[0, 1])

    kv = _dense(x_bf, w_kv_shared)
    kv_p = kv[:n_p].reshape(bp, sp, -1)
    kv_s = kv[n_p:].reshape(bs, ts, -1)

    def window_rows(rows, g, length):
        part = lambda sel: rows[:, rows.shape[1] - length:, (sel * N_GROUPS + g) * ATT_SLOT_DIM:
                                (sel * N_GROUPS + g + 1) * ATT_SLOT_DIM].reshape(
                                    rows.shape[0], length, 1, HEADS_PER_GROUP, ATT_HEAD_DIM)
        return jnp.concatenate([part(0), part(1)], axis=2)

    new_kv_p = [window_rows(kv_p, g, min(GROUP_WINDOWS[g], sp)) for g in range(N_GROUPS)]
    new_kv_s = [jnp.concatenate([kv_caches[g][:, ts:], window_rows(kv_s, g, ts)], axis=1) for g in range(N_GROUPS)]

    slopes = _alibi_slopes()
    q = _dense(x_bf, w_att_q.reshape(D_MODEL, N_ATT_HEADS * ATT_HEAD_DIM))
    q_s = q[n_p:].reshape(bs, ts, -1)
    kv_s_rows = kv[n_p:].reshape(bs, ts, -1)
    outs_p, lses_p, outs_s, lses_s = [], [], [], []
    for g in range(N_GROUPS):
        o_g, lse_g = _att_prompt(q, kv, slopes, group=g, batch=bp, seq=sp)
        outs_p.append(o_g)
        lses_p.append(lse_g)
        o_g, lse_g = _att_sample(q_s, kv_s_rows, kv_caches[g], slopes, group=g)
        outs_s.append(o_g)
        lses_s.append(lse_g)
    att = jnp.concatenate([_att_merge(outs_p, lses_p), _att_merge(outs_s, lses_s)], axis=0)
    mix = _dense(att, w_att_out.reshape(ATT_SLOT_DIM, D_MODEL))
    x, x_bf, logits = _postnorm(x, mix, ln_g[1, 0], ln_b[1, 0], w_router[1], b_router[1])
    ffn = _moe_block(x, x_bf, logits, w_up, b_up, w_down, b_down, 1)
    x, _ = _postnorm(x, ffn, ln_g[1, 1], ln_b[1, 1])

    y_prompt = x[:n_p].reshape(bp, sp, D_MODEL)
    y_sample = x[n_p:].reshape(bs, ts, D_MODEL)
    return (y_prompt, y_sample, s_p[None], cbuf_p[None], new_kv_p[0], new_kv_p[1], new_kv_p[2],
            s_s[None], cbuf_s[None], new_kv_s[0], new_kv_s[1], new_kv_s[2])
```

```python
import functools
import math

import jax
import jax.numpy as jnp
from jax import lax
from jax.experimental import pallas as pl
from jax.experimental.pallas import tpu as pltpu

D_MODEL = 4096
DEPTH = 2
N_A_LAYERS = DEPTH // 2

DN_QK_HEADS = 16
DN_V_HEADS = 32
DN_HEAD_DIM = 128
DN_KEY_DIM = DN_QK_HEADS * DN_HEAD_DIM
DN_VAL_DIM = DN_V_HEADS * DN_HEAD_DIM
DN_CONV_W = 4
DN_CONV_DIM = 2 * DN_KEY_DIM + DN_VAL_DIM
DN_IN_DIM = DN_CONV_DIM + DN_VAL_DIM + 2 * DN_V_HEADS
DN_CHUNK = 64

GROUP_WINDOWS = (128, 512, 2048)
GROUP_DILATIONS = (1, 4, 16)
N_GROUPS = 3
HEADS_PER_GROUP = 16
ATT_HEAD_DIM = 128
N_ATT_HEADS = N_GROUPS * HEADS_PER_GROUP
ATT_SLOT_DIM = HEADS_PER_GROUP * ATT_HEAD_DIM
ALIBI_MAX_BIAS = 8.0

N_EXPERTS = 32
TOP_K = 4
D_EXPERT = D_MODEL // 2
SWIGLU_LIMIT = 7.0
SWIGLU_ALPHA = 1.702

LN_EPS = 1e-5
NORM_EPS = 1e-6
DEEP_ALPHA = (2.0 * DEPTH) ** 0.25

LANES = 128
SUBLANES = 8
MXU_DIM = 256
VMEM_LIMIT_BYTES = 56 * 1024 * 1024

DENSE_TM = 688
DENSE_TN = 512
MOE_TM = 256
MOE_UP_TN = 1024
MOE_DOWN_TN = 1024
ROUTE_BLOCK = 256

BF16 = jnp.bfloat16
F32 = jnp.float32


def _dense_kernel(x_ref, w_ref, o_ref, wbf_ref):
    @pl.when(pl.program_id(1) == 0)
    def _():
        wbf_ref[...] = w_ref[...].astype(BF16)

    o_ref[...] = jnp.dot(x_ref[...], wbf_ref[...], preferred_element_type=F32).astype(o_ref.dtype)


def _dense(x, w, *, col_block_offset=0, n_cols=None, tn=DENSE_TN, out_dtype=F32):
    m, k = x.shape
    n_cols = w.shape[1] if n_cols is None else n_cols
    tm = DENSE_TM if m % DENSE_TM == 0 else m
    assert m % tm == 0 and n_cols % tn == 0
    return pl.pallas_call(
        _dense_kernel,
        grid=(n_cols // tn, m // tm),
        in_specs=[pl.BlockSpec((tm, k), lambda j, i: (i, 0)),
                  pl.BlockSpec((k, tn), lambda j, i: (0, j + col_block_offset))],
        out_specs=pl.BlockSpec((tm, tn), lambda j, i: (i, j)),
        out_shape=jax.ShapeDtypeStruct((m, n_cols), out_dtype),
        scratch_shapes=[pltpu.VMEM((k, tn), BF16)],
        compiler_params=pltpu.CompilerParams(dimension_semantics=("arbitrary", "arbitrary"),
                                             vmem_limit_bytes=VMEM_LIMIT_BYTES),
        name="dense_proj",
    )(x, w)


def _moe_rows(n_tokens):
    n_assign = n_tokens * TOP_K
    n_tiles = -(-(n_assign + N_EXPERTS * (MOE_TM - 1)) // MOE_TM)
    return n_tiles, n_tiles * MOE_TM


def _route(logits):
    n_tokens = logits.shape[0]
    n_tiles, n_rows = _moe_rows(n_tokens)
    top_vals, top_idx = lax.top_k(logits, TOP_K)
    gates = jax.nn.softmax(top_vals, axis=-1)
    flat_e = top_idx.reshape(-1).astype(jnp.int32)
    n_assign = flat_e.shape[0]
    assert n_assign % ROUTE_BLOCK == 0
    onehot = (flat_e[:, None] == jnp.arange(N_EXPERTS, dtype=jnp.int32)[None, :]).astype(F32)
    onehot = onehot.reshape(n_assign // ROUTE_BLOCK, ROUTE_BLOCK, N_EXPERTS)
    tri = (jnp.arange(ROUTE_BLOCK)[:, None] >= jnp.arange(ROUTE_BLOCK)[None, :]).astype(F32)
    within = jnp.einsum('ij,bjk->bik', tri, onehot)
    block_total = within[:, -1, :]
    block_end = jnp.cumsum(block_total, axis=0)
    csum = (within + (block_end - block_total)[:, None, :]).reshape(n_assign, N_EXPERTS)
    rank = jnp.take_along_axis(csum, flat_e[:, None], axis=1)[:, 0].astype(jnp.int32) - 1
    counts = block_end[-1].astype(jnp.int32)
    padded = ((counts + MOE_TM - 1) // MOE_TM) * MOE_TM
    pend = jnp.cumsum(padded)
    pstart = pend - padded
    row_of = pstart[flat_e] + rank
    n_used = (pend[-1] // MOE_TM).astype(jnp.int32)
    tile_start = jnp.minimum(jnp.arange(n_tiles, dtype=jnp.int32), n_used - 1) * MOE_TM
    tile_expert = jnp.minimum(jnp.searchsorted(pend, tile_start, side="right"), N_EXPERTS - 1).astype(jnp.int32)
    order = jnp.argsort(flat_e, stable=True).astype(jnp.int32)
    rows = jnp.arange(n_rows, dtype=jnp.int32)
    e_row = tile_expert[rows // MOE_TM]
    offset = rows - pstart[e_row]
    src = order[jnp.clip((jnp.cumsum(counts) - counts)[e_row] + offset, 0, n_assign - 1)] // TOP_K
    src_tok = jnp.where(offset < counts[e_row], src, 0)
    return row_of.reshape(n_tokens, TOP_K), src_tok, gates, tile_expert, n_used.reshape(1)


def _weights_changed(te_ref, i):
    return (i == 0) | (te_ref[i] != te_ref[jnp.maximum(i - 1, 0)])


def _moe_up_kernel(te_ref, nu_ref, x_ref, w_ref, b_ref, o_ref, wbf_ref):
    i = pl.program_id(1)

    @pl.when(_weights_changed(te_ref, i))
    def _():
        wbf_ref[...] = w_ref[0, 0].astype(BF16)

    @pl.when(i < nu_ref[0])
    def _():
        h = jnp.dot(x_ref[...], wbf_ref[...], preferred_element_type=F32) + b_ref[0]
        glu = jnp.minimum(h, SWIGLU_LIMIT)
        glu = glu * jax.nn.sigmoid(SWIGLU_ALPHA * glu)
        lin = jnp.clip(h, -SWIGLU_LIMIT, SWIGLU_LIMIT) + 1.0
        tn = h.shape[1]
        prod = glu * pltpu.roll(lin, tn - 1, 1)
        lane = lax.broadcasted_iota(jnp.int32, prod.shape, 1)
        prod = jnp.where(lane % 2 == 0, prod, 0.0).astype(BF16)
        r = lax.broadcasted_iota(jnp.int32, (MXU_DIM, MXU_DIM // 2), 0)
        c = lax.broadcasted_iota(jnp.int32, (MXU_DIM, MXU_DIM // 2), 1)
        sel = (r == 2 * c).astype(BF16)
        for q in range(tn // MXU_DIM):
            part = jnp.dot(prod[:, q * MXU_DIM:(q + 1) * MXU_DIM], sel, preferred_element_type=F32)
            o_ref[:, q * (MXU_DIM // 2):(q + 1) * (MXU_DIM // 2)] = part.astype(o_ref.dtype)


def _moe_down_kernel(te_ref, nu_ref, h_ref, w_ref, b_ref, o_ref, wbf_ref):
    i = pl.program_id(1)

    @pl.when(_weights_changed(te_ref, i))
    def _():
        wbf_ref[...] = w_ref[0, 0].astype(BF16)

    @pl.when(i < nu_ref[0])
    def _():
        o_ref[...] = jnp.dot(h_ref[...], wbf_ref[...], preferred_element_type=F32) + b_ref[0]


def _moe_experts(x_rows, tile_expert, n_used, w_up, b_up, w_down, b_down, layer):
    n_rows = x_rows.shape[0]
    n_tiles = n_rows // MOE_TM
    tm = MOE_TM

    def row_tile(j, i, te, nu):
        return (jnp.minimum(i, nu[0] - 1), 0)

    tn = MOE_UP_TN
    hact = pl.pallas_call(
        _moe_up_kernel,
        grid_spec=pltpu.PrefetchScalarGridSpec(
            num_scalar_prefetch=2,
            grid=(2 * D_EXPERT // tn, n_tiles),
            in_specs=[pl.BlockSpec((tm, D_MODEL), row_tile),
                      pl.BlockSpec((1, 1, D_MODEL, tn), lambda j, i, te, nu: (layer, te[i], 0, j)),
                      pl.BlockSpec((1, 1, tn), lambda j, i, te, nu: (layer * N_EXPERTS + te[i], 0, j))],
            out_specs=pl.BlockSpec((tm, tn // 2), lambda j, i, te, nu: (jnp.minimum(i, nu[0] - 1), j)),
            scratch_shapes=[pltpu.VMEM((D_MODEL, tn), BF16)]),
        out_shape=jax.ShapeDtypeStruct((n_rows, D_EXPERT), BF16),
        compiler_params=pltpu.CompilerParams(dimension_semantics=("arbitrary", "arbitrary"),
                                             vmem_limit_bytes=VMEM_LIMIT_BYTES),
        name="moe_up_swiglu",
    )(tile_expert, n_used, x_rows, w_up, b_up.reshape(DEPTH * N_EXPERTS, 1, 2 * D_EXPERT))

    tn = MOE_DOWN_TN
    return pl.pallas_call(
        _moe_down_kernel,
        grid_spec=pltpu.PrefetchScalarGridSpec(
            num_scalar_prefetch=2,
            grid=(D_MODEL // tn, n_tiles),
            in_specs=[pl.BlockSpec((tm, D_EXPERT), row_tile),
                      pl.BlockSpec((1, 1, D_EXPERT, tn), lambda j, i, te, nu: (layer, te[i], 0, j)),
                      pl.BlockSpec((1, 1, tn), lambda j, i, te, nu: (layer * N_EXPERTS + te[i], 0, j))],
            out_specs=pl.BlockSpec((tm, tn), lambda j, i, te, nu: (jnp.minimum(i, nu[0] - 1), j)),
            scratch_shapes=[pltpu.VMEM((D_EXPERT, tn), BF16)]),
        out_shape=jax.ShapeDtypeStruct((n_rows, D_MODEL), F32),
        compiler_params=pltpu.CompilerParams(dimension_semantics=("arbitrary", "arbitrary"),
                                             vmem_limit_bytes=VMEM_LIMIT_BYTES),
        name="moe_down",
    )(tile_expert, n_used, hact, w_down, b_down.reshape(DEPTH * N_EXPERTS, 1, D_MODEL))


DN_VH_STEP = 8
DN_QKH_STEP = DN_VH_STEP // (DN_V_HEADS // DN_QK_HEADS)
DN_GROUPS = DN_V_HEADS // DN_VH_STEP
DN_BASE_BLOCK = 16


def _dot(a, b):
    return jnp.dot(a, b, preferred_element_type=F32)


def _dot_nt(a, b):
    return lax.dot_general(a, b, (((1,), (1,)), ((), ())), preferred_element_type=F32)


def _split(a):
    hi = a.astype(BF16)
    return hi, (a - hi.astype(F32)).astype(BF16)


def _dot3(a, b, dot=_dot):
    return dot(a[0], b[0]) + (dot(a[0], b[1]) + dot(a[1], b[0]))


def _unit_lower_inverses(lows, ii, jj):
    eye = (ii == jj).astype(F32)
    blk16 = (ii // DN_BASE_BLOCK) == (jj // DN_BASE_BLOCK)
    blk32 = (ii // (2 * DN_BASE_BLOCK)) == (jj // (2 * DN_BASE_BLOCK))
    ps = [jnp.where(blk16, -low, 0.0) for low in lows]
    ts = [eye + p for p in ps]
    for _ in range(3):
        pss = [_split(p) for p in ps]
        ps = [_dot3(s, s) for s in pss]
        ts = [t + _dot3(_split(t), _split(p)) for t, p in zip(ts, ps)]
    for sel in (blk32 & ~blk16, ~blk32):
        tss = [_split(t) for t in ts]
        us = [_dot3(_split(jnp.where(sel, low, 0.0)), s) for low, s in zip(lows, tss)]
        ts = [t - _dot3(s, _split(u)) for t, s, u in zip(ts, tss, us)]
    return ts


def _dn_kernel(q_ref, k_ref, v_ref, z_ref, gate_ref, cq_ref, ck_ref, cv_ref, s0_ref, wq_ref, wk_ref, wv_ref,
               hp_ref, nw_ref, o_ref, s_ref, pq_ref, pk_ref, pv_ref, *, n_valid):
    c = DN_CHUNK
    hd = DN_HEAD_DIM

    @pl.when(pl.program_id(2) == 0)
    def _():
        pq_ref[c - SUBLANES:c, :] = cq_ref[0]
        pk_ref[c - SUBLANES:c, :] = ck_ref[0]
        pv_ref[c - SUBLANES:c, :] = cv_ref[0]
        s_ref[...] = s0_ref[...]

    row = lax.broadcasted_iota(jnp.int32, (c, 1), 0)
    ii = lax.broadcasted_iota(jnp.int32, (c, c), 0)
    jj = lax.broadcasted_iota(jnp.int32, (c, c), 1)

    def conv_silu(u_ref, prev_ref, w_ref):
        u = u_ref[...]
        prev = prev_ref[...]
        y = u * w_ref[DN_CONV_W - 1:DN_CONV_W, :]
        for s in range(1, DN_CONV_W):
            shifted = jnp.where(row >= s, pltpu.roll(u, s, 0), pltpu.roll(prev, s, 0))
            y = y + shifted * w_ref[DN_CONV_W - 1 - s:DN_CONV_W - s, :]
        prev_ref[...] = u
        return y * jax.nn.sigmoid(y)

    q_all = conv_silu(q_ref, pq_ref, wq_ref)
    k_all = conv_silu(k_ref, pk_ref, wk_ref)
    v_all = conv_silu(v_ref, pv_ref, wv_ref)

    gt = gate_ref[...]
    beta_all = jax.nn.sigmoid(gt)
    pre = gt + hp_ref[0, 0:1, :]
    softplus = jnp.maximum(pre, 0.0) + jnp.log1p(jnp.exp(-jnp.abs(pre)))
    g_all = -jnp.exp(hp_ref[0, 1:2, :]) * softplus
    if n_valid < c:
        live = row < n_valid
        beta_all = jnp.where(live, beta_all, 0.0)
        g_all = jnp.where(live, g_all, 0.0)
        k_all = jnp.where(live, k_all, 0.0)
        v_all = jnp.where(live, v_all, 0.0)
    gc = g_all
    s = 1
    while s < c:
        gc = gc + jnp.where(row >= s, pltpu.roll(gc, s, 0), 0.0)
        s *= 2
    gc_t = gc.T

    def l2n(t):
        return t * lax.rsqrt(jnp.sum(t * t, axis=-1, keepdims=True) + NORM_EPS)

    heads = range(DN_VH_STEP)
    rep = DN_VH_STEP // DN_QKH_STEP
    q_n = [l2n(q_all[:, j * hd:(j + 1) * hd]) * hd ** -0.5 for j in range(DN_QKH_STEP)]
    k_n = [l2n(k_all[:, j * hd:(j + 1) * hd]) for j in range(DN_QKH_STEP)]
    kk_qk = [_dot3(_split(jnp.concatenate([k, q], axis=0)), _split(k), _dot_nt) for k, q in zip(k_n, q_n)]
    g_col = [gc[:, SUBLANES + h:SUBLANES + h + 1] for h in heads]
    beta = [beta_all[:, h:h + 1] for h in heads]
    decay = [jnp.exp(jnp.minimum(g_col[h] - gc_t[SUBLANES + h:SUBLANES + h + 1, :], 0.0)) for h in heads]
    lows = [jnp.where(ii > jj, beta[h] * kk_qk[h // rep][:c] * decay[h], 0.0) for h in heads]
    t_inv = _unit_lower_inverses(lows, ii, jj)
    e_col = [jnp.exp(g) for g in g_col]
    rhs = [jnp.concatenate([v_all[:, h * hd:(h + 1) * hd] * beta[h], k_n[h // rep] * (beta[h] * e_col[h])], axis=1)
           for h in heads]
    w = [_dot3(_split(t_inv[h]), _split(rhs[h])) for h in heads]
    states = [s_ref[0, h] for h in heads]
    ks_qs = [_dot3(_split(jnp.concatenate([w[h][:, hd:], q_n[h // rep] * e_col[h]], axis=0)), _split(states[h]))
             for h in heads]
    v_new = [w[h][:, :hd] - ks_qs[h][:c] for h in heads]
    g_last = [g[c - 1:c, :] for g in g_col]
    k_tail = [k_n[h // rep] * jnp.exp(g_last[h] - g_col[h]) for h in heads]
    for h in heads:
        s_ref[0, h] = states[h] * jnp.exp(g_last[h]) + _dot3(_split(k_tail[h].T), _split(v_new[h]))
    attn = [jnp.where(ii >= jj, kk_qk[h // rep][c:] * decay[h], 0.0) for h in heads]
    outs = [ks_qs[h][c:] + _dot(attn[h].astype(BF16), v_new[h].astype(BF16)) for h in heads]
    for h in heads:
        o = outs[h]
        o = o * lax.rsqrt(jnp.mean(o * o, axis=-1, keepdims=True) + NORM_EPS) * nw_ref[...]
        z_h = z_ref[:, h * hd:(h + 1) * hd]
        o_ref[:, h * hd:(h + 1) * hd] = (o * (z_h * jax.nn.sigmoid(z_h))).astype(o_ref.dtype)


def _deltanet(proj, gates, conv_init, s0, w_conv, head_params, norm_w, *, batch, n_chunks, n_valid):
    nc = n_chunks
    qw = DN_QKH_STEP * DN_HEAD_DIM
    vw = DN_VH_STEP * DN_HEAD_DIM
    k_off = DN_KEY_DIM // qw
    v_off = 2 * DN_KEY_DIM // vw
    z_off = DN_CONV_DIM // vw
    chunk = lambda width, off: pl.BlockSpec((DN_CHUNK, width), lambda bi, g, ci: (bi * nc + ci, off + g))
    init = lambda width, off: pl.BlockSpec((1, SUBLANES, width), lambda bi, g, ci: (bi, 0, off + g))
    wspec = lambda width, off: pl.BlockSpec((DN_CONV_W, width), lambda bi, g, ci: (0, off + g))
    state = pl.BlockSpec((1, DN_VH_STEP, DN_HEAD_DIM, DN_HEAD_DIM), lambda bi, g, ci: (bi, g, 0, 0))
    return pl.pallas_call(
        functools.partial(_dn_kernel, n_valid=n_valid),
        grid=(batch, DN_GROUPS, nc),
        in_specs=[chunk(qw, 0), chunk(qw, k_off), chunk(vw, v_off), chunk(vw, z_off), chunk(LANES, 0),
                  init(qw, 0), init(qw, k_off), init(vw, v_off), state,
                  wspec(qw, 0), wspec(qw, k_off), wspec(vw, v_off),
                  pl.BlockSpec((1, SUBLANES, LANES), lambda bi, g, ci: (g, 0, 0)),
                  pl.BlockSpec((1, DN_HEAD_DIM), lambda bi, g, ci: (0, 0))],
        out_specs=[chunk(vw, 0), state],
        out_shape=[jax.ShapeDtypeStruct((batch * nc * DN_CHUNK, DN_VAL_DIM), BF16),
                   jax.ShapeDtypeStruct((batch, DN_V_HEADS, DN_HEAD_DIM, DN_HEAD_DIM), F32)],
        scratch_shapes=[pltpu.VMEM((DN_CHUNK, qw), F32), pltpu.VMEM((DN_CHUNK, qw), F32),
                        pltpu.VMEM((DN_CHUNK, vw), F32)],
        compiler_params=pltpu.CompilerParams(dimension_semantics=("arbitrary", "arbitrary", "arbitrary"),
                                             vmem_limit_bytes=VMEM_LIMIT_BYTES),
        name="deltanet_chunk",
    )(proj, proj, proj, proj, gates, conv_init, conv_init, conv_init, s0, w_conv, w_conv, w_conv,
      head_params, norm_w.reshape(1, DN_HEAD_DIM))


def _dn_gate_layout(tail):
    lead = tail.shape[:-1]
    t = tail.reshape(lead + (2, DN_GROUPS, DN_VH_STEP))
    t = jnp.swapaxes(t, -3, -2).reshape(lead + (DN_GROUPS, 2 * DN_VH_STEP))
    t = jnp.pad(t, [(0, 0)] * (len(lead) + 1) + [(0, LANES - 2 * DN_VH_STEP)])
    return t.reshape(lead + (DN_GROUPS * LANES,))


def _dn_head_params(a_log, dt_bias):
    def lay(p):
        p = p.reshape(DN_GROUPS, 1, DN_VH_STEP)
        return jnp.pad(p, ((0, 0), (0, 0), (DN_VH_STEP, LANES - 2 * DN_VH_STEP)))
    rows = jnp.concatenate([lay(dt_bias), lay(a_log)], axis=1)
    return jnp.pad(rows, ((0, 0), (0, SUBLANES - 2), (0, 0)))


ATT_BLK = 128
ATT_MASKED = -1e30


ATT_P_HEADS = (16, 8, 4)
ATT_CLASS_UNROLL = 16


def _att_prompt_kernel(slope_ref, q_ref, *refs, group, dilation, heads, has_prev):
    n_in = 4 if has_prev else 2
    kv_refs, (o_ref, lse_ref), stage = refs[:n_in], refs[n_in:n_in + 2], refs[n_in + 2:]
    hd = ATT_HEAD_DIM
    blk = ATT_BLK
    n_keys = 2 * blk if has_prev else blk
    qi = lax.broadcasted_iota(jnp.int32, (blk, n_keys), 0)
    kj = lax.broadcasted_iota(jnp.int32, (blk, n_keys), 1)
    dist = qi + (n_keys - blk) - kj
    valid = (dist >= 0) & (dist <= blk)
    if has_prev:
        valid = valid & ((kj >= blk) | (pl.program_id(1) > 0))
    dist_f = (dist * dilation).astype(F32)
    lane = lax.broadcasted_iota(jnp.int32, (blk, LANES), 1)
    head0 = pl.program_id(2) * heads

    @pl.when(pl.program_id(2) == 0)
    def _():
        lse_ref[...] = jnp.zeros(lse_ref.shape, F32)

    for h in range(heads):
        hs = slice(h * hd, (h + 1) * hd)
        slope = slope_ref[group, head0 + h]
        if dilation > 1:
            q_st, o_st = stage[0], stage[1]
            q_st[...] = q_ref[:, hs]
            for src, dst in zip(kv_refs, stage[2:]):
                dst[...] = src[:, hs]
            kv_src = stage[2:]
        else:
            kv_src = kv_refs

        def one_class(r, carry):
            if dilation > 1:
                rows = pl.ds(r, blk, stride=dilation)
                q = q_st[rows, :]
                parts = [ref[rows, :] for ref in kv_src]
            else:
                rows = pl.ds(0, blk)
                q = q_ref[:, hs]
                parts = [ref[:, hs] for ref in kv_src]
            if has_prev:
                k = jnp.concatenate([parts[1], parts[0]], axis=0)
                v = jnp.concatenate([parts[3], parts[2]], axis=0)
            else:
                k, v = parts
            sc = _dot_nt(q.astype(BF16), k.astype(BF16)) * hd ** -0.5 - slope * dist_f
            sc = jnp.where(valid, sc, ATT_MASKED)
            m = jnp.max(sc, axis=-1, keepdims=True)
            p = jnp.exp(sc - m)
            l = jnp.sum(p, axis=-1, keepdims=True)
            o = _dot(p.astype(BF16), v.astype(BF16)) / l
            if dilation > 1:
                o_st[rows, :] = o
            else:
                o_ref[:, hs] = o
            lse_ref[rows, :] = jnp.where(lane == head0 + h, m + jnp.log(l), lse_ref[rows, :])
            return carry

        if dilation > 1:
            lax.fori_loop(0, dilation, one_class, 0, unroll=ATT_CLASS_UNROLL)
            o_ref[:, hs] = o_st[...]
        else:
            one_class(0, 0)


def _att_prompt(q, kv, slopes, *, group, batch, seq):
    d = GROUP_DILATIONS[group]
    heads = ATT_P_HEADS[group]
    span = d * ATT_BLK
    assert GROUP_WINDOWS[group] // d == ATT_BLK and seq % span == 0 and HEADS_PER_GROUP % heads == 0
    ns = seq // span
    has_prev = ns > 1
    w = heads * ATT_HEAD_DIM
    n_hc = ATT_SLOT_DIM // w
    n_kv = kv.shape[1] // w
    cur = lambda col: pl.BlockSpec((span, w), lambda b, n, hc: (b * ns + n, col + hc))
    prev = lambda col: pl.BlockSpec((span, w), lambda b, n, hc: (b * ns + jnp.maximum(n - 1, 0), col + hc))
    k_col, v_col = group * n_hc, n_kv // 2 + group * n_hc
    in_specs = [pl.BlockSpec(memory_space=pltpu.SMEM), cur(group * n_hc)]
    in_specs += [cur(k_col), prev(k_col), cur(v_col), prev(v_col)] if has_prev else [cur(k_col), cur(v_col)]
    return pl.pallas_call(
        functools.partial(_att_prompt_kernel, group=group, dilation=d, heads=heads, has_prev=has_prev),
        grid=(batch, ns, n_hc),
        in_specs=in_specs,
        out_specs=[cur(0), pl.BlockSpec((span, LANES), lambda b, n, hc: (b * ns + n, 0))],
        out_shape=[jax.ShapeDtypeStruct((batch * seq, ATT_SLOT_DIM), F32),
                   jax.ShapeDtypeStruct((batch * seq, LANES), F32)],
        scratch_shapes=[pltpu.VMEM((span, ATT_HEAD_DIM), F32)] * ((4 + 2 * has_prev) if d > 1 else 0),
        compiler_params=pltpu.CompilerParams(dimension_semantics=("arbitrary", "arbitrary", "arbitrary"),
                                             vmem_limit_bytes=VMEM_LIMIT_BYTES),
        name="dilated_attention_prompt",
    )(slopes, q, *([kv] * (4 if has_prev else 2)))


ATT_S_HEADS = 4


def _att_sample_kernel(slope_ref, q_ref, kc_ref, vc_ref, kn_ref, vn_ref, o_ref, lse_ref, *, group, dilation):
    hd = ATT_HEAD_DIM
    t_new = q_ref.shape[1]
    l_buf = kc_ref.shape[1]
    window = GROUP_WINDOWS[group]
    tq = lax.broadcasted_iota(jnp.int32, (t_new, l_buf), 0)
    ic = lax.broadcasted_iota(jnp.int32, (t_new, l_buf), 1)
    dist_c = l_buf + tq - ic
    ok_c = (((ic - tq) & (dilation - 1)) == 0) & (dist_c <= window)
    tn = lax.broadcasted_iota(jnp.int32, (t_new, t_new), 0)
    un = lax.broadcasted_iota(jnp.int32, (t_new, t_new), 1)
    dist_n = tn - un
    ok_n = (dist_n >= 0) & ((dist_n & (dilation - 1)) == 0) & (dist_n <= window)
    lane = lax.broadcasted_iota(jnp.int32, (t_new, LANES), 1)
    lse_all = jnp.zeros((t_new, LANES), F32)
    for h in range(ATT_S_HEADS):
        hs = slice(h * hd, (h + 1) * hd)
        slope = slope_ref[group, pl.program_id(1) * ATT_S_HEADS + h]
        q = q_ref[0, :, hs].astype(BF16)
        sc_c = _dot_nt(q, kc_ref[0, :, hs].astype(BF16)) * hd ** -0.5 - slope * dist_c.astype(F32)
        sc_n = _dot_nt(q, kn_ref[0, :, hs].astype(BF16)) * hd ** -0.5 - slope * dist_n.astype(F32)
        sc_c = jnp.where(ok_c, sc_c, ATT_MASKED)
        sc_n = jnp.where(ok_n, sc_n, ATT_MASKED)
        m = jnp.maximum(jnp.max(sc_c, axis=-1, keepdims=True), jnp.max(sc_n, axis=-1, keepdims=True))
        p_c = jnp.exp(sc_c - m)
        p_n = jnp.exp(sc_n - m)
        l = jnp.sum(p_c, axis=-1, keepdims=True) + jnp.sum(p_n, axis=-1, keepdims=True)
        acc = (_dot(p_c.astype(BF16), vc_ref[0, :, hs].astype(BF16))
               + _dot(p_n.astype(BF16), vn_ref[0, :, hs].astype(BF16)))
        o_ref[0, :, hs] = acc / l
        lse_all = jnp.where(lane == h, m + jnp.log(l), lse_all)
    lse_ref[0, 0] = lse_all


def _att_sample(q, kv_new, cache, slopes, *, group):
    b, t, _ = q.shape
    l_buf = cache.shape[1]
    w = ATT_S_HEADS * ATT_HEAD_DIM
    per_group = ATT_SLOT_DIM // w
    n_q = q.shape[2] // w
    cache2 = cache.reshape(b, l_buf, 2 * ATT_SLOT_DIM)
    new = lambda col: pl.BlockSpec((1, t, w), lambda bi, hc: (bi, 0, col + hc))
    buf = lambda col: pl.BlockSpec((1, l_buf, w), lambda bi, hc: (bi, 0, col + hc))
    o, lse = pl.pallas_call(
        functools.partial(_att_sample_kernel, group=group, dilation=GROUP_DILATIONS[group]),
        grid=(b, per_group),
        in_specs=[pl.BlockSpec(memory_space=pltpu.SMEM), new(group * per_group), buf(0), buf(per_group),
                  new(group * per_group), new(n_q + group * per_group)],
        out_specs=[new(0), pl.BlockSpec((1, 1, t, LANES), lambda bi, hc: (bi, hc, 0, 0))],
        out_shape=[jax.ShapeDtypeStruct((b, t, ATT_SLOT_DIM), F32),
                   jax.ShapeDtypeStruct((b, per_group, t, LANES), F32)],
        compiler_params=pltpu.CompilerParams(dimension_semantics=("arbitrary", "arbitrary"),
                                             vmem_limit_bytes=VMEM_LIMIT_BYTES),
        name="dilated_attention_sample",
    )(slopes, q, cache2, cache2, kv_new, kv_new)
    lse = lse[..., :ATT_S_HEADS].transpose(0, 2, 1, 3).reshape(b * t, HEADS_PER_GROUP)
    return o.reshape(b * t, ATT_SLOT_DIM), jnp.pad(lse, ((0, 0), (0, LANES - HEADS_PER_GROUP)))


def _att_merge_kernel(o0_ref, o1_ref, o2_ref, l0_ref, l1_ref, l2_ref, out_ref):
    hd = ATT_HEAD_DIM
    lses = [l0_ref[...], l1_ref[...], l2_ref[...]]
    m = jnp.maximum(jnp.maximum(lses[0], lses[1]), lses[2])
    zs = [jnp.exp(l - m) for l in lses]
    inv = 1.0 / (zs[0] + zs[1] + zs[2])
    for h in range(HEADS_PER_GROUP):
        hs = slice(h * hd, (h + 1) * hd)
        acc = (zs[0][:, h:h + 1] * inv[:, h:h + 1]) * o0_ref[:, hs]
        acc = acc + (zs[1][:, h:h + 1] * inv[:, h:h + 1]) * o1_ref[:, hs]
        acc = acc + (zs[2][:, h:h + 1] * inv[:, h:h + 1]) * o2_ref[:, hs]
        out_ref[:, hs] = acc.astype(out_ref.dtype)


def _att_merge(outs, lses):
    n = outs[0].shape[0]
    tm = 256 if n % 256 == 0 else n
    ospec = pl.BlockSpec((tm, ATT_SLOT_DIM), lambda i: (i, 0))
    lspec = pl.BlockSpec((tm, LANES), lambda i: (i, 0))
    return pl.pallas_call(
        _att_merge_kernel,
        grid=(n // tm,),
        in_specs=[ospec] * 3 + [lspec] * 3,
        out_specs=ospec,
        out_shape=jax.ShapeDtypeStruct((n, ATT_SLOT_DIM), BF16),
        compiler_params=pltpu.CompilerParams(dimension_semantics=("arbitrary",), vmem_limit_bytes=VMEM_LIMIT_BYTES),
        name="attention_group_merge",
    )(*outs, *lses)


NORM_TM = 192


def _postnorm_kernel(x_ref, *rest, n_terms, with_router):
    if n_terms:
        gates = rest[n_terms][...]
        f = gates[:, 0:1] * rest[0][...]
        for k in range(1, n_terms):
            f = f + gates[:, k:k + 1] * rest[k][...]
        rest = rest[n_terms + 1:]
    else:
        f = rest[0][...]
        rest = rest[1:]
    g_ref, b_ref = rest[:2]
    rest = rest[2:]
    y = DEEP_ALPHA * x_ref[...] + f
    mu = jnp.mean(y, axis=-1, keepdims=True)
    yc = y - mu
    var = jnp.mean(yc * yc, axis=-1, keepdims=True)
    out = yc * lax.rsqrt(var + LN_EPS) * g_ref[...] + b_ref[...]
    if with_router:
        wr_ref, br_ref, o_ref, obf_ref, lg_ref = rest
        lg_ref[...] = jnp.dot(out, wr_ref[...], precision=lax.Precision.HIGHEST,
                              preferred_element_type=F32) + br_ref[...]
    else:
        o_ref, obf_ref = rest
    o_ref[...] = out
    obf_ref[...] = out.astype(BF16)


def _postnorm(x, f, g, b, w_router=None, b_router=None, gates=None):
    n, d = x.shape
    tm = NORM_TM if n % NORM_TM == 0 else n
    row = pl.BlockSpec((tm, d), lambda i: (i, 0))
    vec = pl.BlockSpec((1, d), lambda i: (0, 0))
    with_router = w_router is not None
    n_terms = 0 if gates is None else len(f)
    if n_terms:
        in_specs = [row] * (1 + n_terms) + [pl.BlockSpec((tm, n_terms), lambda i: (i, 0)), vec, vec]
        args = [x, *f, gates, g.reshape(1, d), b.reshape(1, d)]
    else:
        in_specs = [row, row, vec, vec]
        args = [x, f, g.reshape(1, d), b.reshape(1, d)]
    out_specs = [row, row]
    out_shape = [jax.ShapeDtypeStruct((n, d), F32), jax.ShapeDtypeStruct((n, d), BF16)]
    if with_router:
        in_specs += [pl.BlockSpec((d, N_EXPERTS), lambda i: (0, 0)), pl.BlockSpec((1, N_EXPERTS), lambda i: (0, 0))]
        args += [w_router, b_router.reshape(1, N_EXPERTS)]
        out_specs.append(pl.BlockSpec((tm, N_EXPERTS), lambda i: (i, 0)))
        out_shape.append(jax.ShapeDtypeStruct((n, N_EXPERTS), F32))
    return pl.pallas_call(
        functools.partial(_postnorm_kernel, n_terms=n_terms, with_router=with_router),
        grid=(n // tm,),
        in_specs=in_specs, out_specs=out_specs, out_shape=out_shape,
        compiler_params=pltpu.CompilerParams(dimension_semantics=("arbitrary",), vmem_limit_bytes=VMEM_LIMIT_BYTES),
        name="postnorm_router" if with_router else "postnorm",
    )(*args)


def _alibi_slopes():
    h = jnp.arange(1, N_ATT_HEADS + 1, dtype=F32)
    return (2.0 ** (-ALIBI_MAX_BIAS * h / N_ATT_HEADS)).reshape(N_GROUPS, HEADS_PER_GROUP)


def _moe_block(x_bf, logits, w_up, b_up, w_down, b_down, layer):
    row_of, src_tok, gates, tile_expert, n_used = _route(logits)
    x_rows = jnp.take(x_bf, src_tok, axis=0, mode="clip")
    y_rows = _moe_experts(x_rows, tile_expert, n_used, w_up, b_up, w_down, b_down, layer)
    return [jnp.take(y_rows, row_of[:, k], axis=0, mode="clip") for k in range(TOP_K)], gates


def kernel(x_prompt, x_sample, state_dn_S, state_dn_conv, cache_kv_w128, cache_kv_w512, cache_kv_w2048,
           w_dn_in, w_dn_conv, dn_a_log, dn_dt_bias, dn_norm_w, w_dn_out, w_kv_shared, w_att_q, w_att_out,
           ln_g, ln_b, w_router, b_router, w_up, b_up, w_down, b_down):
    bp, sp, _ = x_prompt.shape
    bs, ts, _ = x_sample.shape
    n_p, n_s = bp * sp, bs * ts
    kv_caches = (cache_kv_w128, cache_kv_w512, cache_kv_w2048)
    x = jnp.concatenate([x_prompt.reshape(n_p, D_MODEL), x_sample.reshape(n_s, D_MODEL)], axis=0)

    w_in = w_dn_in.reshape(D_MODEL, DN_IN_DIM)
    x_bf = x.astype(BF16)
    n_main = DN_CONV_DIM + DN_VAL_DIM
    proj = _dense(x_bf, w_in, n_cols=n_main)
    tail = _dense(x_bf, w_in, col_block_offset=n_main // LANES, n_cols=LANES, tn=LANES)[:, :2 * DN_V_HEADS]
    gates = _dn_gate_layout(tail)
    head_params = _dn_head_params(dn_a_log[0], dn_dt_bias[0])
    pad_s = DN_CHUNK - ts
    proj_s = proj[n_p:].reshape(bs, ts, n_main)
    o_p, s_p = _deltanet(proj, gates, jnp.zeros((bp, SUBLANES, DN_CONV_DIM), F32),
                         jnp.zeros((bp, DN_V_HEADS, DN_HEAD_DIM, DN_HEAD_DIM), F32), w_dn_conv[0], head_params,
                         dn_norm_w[0], batch=bp, n_chunks=sp // DN_CHUNK, n_valid=DN_CHUNK)
    o_s, s_s = _deltanet(jnp.pad(proj_s, ((0, 0), (0, pad_s), (0, 0))).reshape(bs * DN_CHUNK, n_main),
                         jnp.pad(gates[n_p:].reshape(bs, ts, -1), ((0, 0), (0, pad_s), (0, 0))).reshape(bs * DN_CHUNK, -1),
                         jnp.pad(state_dn_conv[0], ((0, 0), (SUBLANES - (DN_CONV_W - 1), 0), (0, 0))),
                         state_dn_S[0], w_dn_conv[0], head_params, dn_norm_w[0], batch=bs, n_chunks=1, n_valid=ts)
    cbuf_p = jnp.stack([proj[(b + 1) * sp - (DN_CONV_W - 1):(b + 1) * sp, :DN_CONV_DIM] for b in range(bp)])
    cbuf_s = proj_s[:, ts - (DN_CONV_W - 1):, :DN_CONV_DIM]
    o = jnp.concatenate([o_p, o_s.reshape(bs, DN_CHUNK, DN_VAL_DIM)[:, :ts].reshape(n_s, DN_VAL_DIM)], axis=0)
    mix = _dense(o, w_dn_out.reshape(DN_VAL_DIM, D_MODEL))
    x, x_bf, logits = _postnorm(x, mix, ln_g[0, 0], ln_b[0, 0], w_router[0], b_router[0])
    ffn, gates = _moe_block(x_bf, logits, w_up, b_up, w_down, b_down, 0)
    x, x_bf = _postnorm(x, ffn, ln_g[0, 1], ln_b[0, 1], gates=gates)

    kv = _dense(x_bf, w_kv_shared)
    def window_rows(first_row, n_seq, seq_len, g, length):
        def part(b, sel):
            r1 = first_row + (b + 1) * seq_len
            c0 = (sel * N_GROUPS + g) * ATT_SLOT_DIM
            return kv[r1 - length:r1, c0:c0 + ATT_SLOT_DIM].reshape(length, 1, HEADS_PER_GROUP, ATT_HEAD_DIM)
        return jnp.stack([jnp.concatenate([part(b, 0), part(b, 1)], axis=1) for b in range(n_seq)])

    new_kv_p = [window_rows(0, bp, sp, g, min(GROUP_WINDOWS[g], sp)) for g in range(N_GROUPS)]
    new_kv_s = [jnp.concatenate([kv_caches[g][:, ts:], window_rows(n_p, bs, ts, g, ts)], axis=1)
                for g in range(N_GROUPS)]

    slopes = _alibi_slopes()
    q = _dense(x_bf, w_att_q.reshape(D_MODEL, N_ATT_HEADS * ATT_HEAD_DIM))
    q_s = q[n_p:].reshape(bs, ts, -1)
    kv_s_rows = kv[n_p:].reshape(bs, ts, -1)
    outs_p, lses_p, outs_s, lses_s = [], [], [], []
    for g in range(N_GROUPS):
        o_g, lse_g = _att_prompt(q, kv, slopes, group=g, batch=bp, seq=sp)
        outs_p.append(o_g)
        lses_p.append(lse_g)
        o_g, lse_g = _att_sample(q_s, kv_s_rows, kv_caches[g], slopes, group=g)
        outs_s.append(o_g)
        lses_s.append(lse_g)
    att = jnp.concatenate([_att_merge(outs_p, lses_p), _att_merge(outs_s, lses_s)], axis=0)
    mix = _dense(att, w_att_out.reshape(ATT_SLOT_DIM, D_MODEL))
    x, x_bf, logits = _postnorm(x, mix, ln_g[1, 0], ln_b[1, 0], w_router[1], b_router[1])
    ffn, gates = _moe_block(x_bf, logits, w_up, b_up, w_down, b_down, 1)
    x, _ = _postnorm(x, ffn, ln_g[1, 1], ln_b[1, 1], gates=gates)

    y_prompt = x[:n_p].reshape(bp, sp, D_MODEL)
    y_sample = x[n_p:].reshape(bs, ts, D_MODEL)
    return (y_prompt, y_sample, s_p[None], cbuf_p[None], new_kv_p[0], new_kv_p[1], new_kv_p[2],
            s_s[None], cbuf_s[None], new_kv_s[0], new_kv_s[1], new_kv_s[2])
```

```python
import functools
import math

import jax
import jax.numpy as jnp
from jax import lax
from jax.experimental import pallas as pl
from jax.experimental.pallas import tpu as pltpu

D_MODEL = 4096
DEPTH = 2
N_A_LAYERS = DEPTH // 2

DN_QK_HEADS = 16
DN_V_HEADS = 32
DN_HEAD_DIM = 128
DN_KEY_DIM = DN_QK_HEADS * DN_HEAD_DIM
DN_VAL_DIM = DN_V_HEADS * DN_HEAD_DIM
DN_CONV_W = 4
DN_CONV_DIM = 2 * DN_KEY_DIM + DN_VAL_DIM
DN_IN_DIM = DN_CONV_DIM + DN_VAL_DIM + 2 * DN_V_HEADS
DN_CHUNK = 64

GROUP_WINDOWS = (128, 512, 2048)
GROUP_DILATIONS = (1, 4, 16)
N_GROUPS = 3
HEADS_PER_GROUP = 16
ATT_HEAD_DIM = 128
N_ATT_HEADS = N_GROUPS * HEADS_PER_GROUP
ATT_SLOT_DIM = HEADS_PER_GROUP * ATT_HEAD_DIM
ALIBI_MAX_BIAS = 8.0

N_EXPERTS = 32
TOP_K = 4
D_EXPERT = D_MODEL // 2
SWIGLU_LIMIT = 7.0
SWIGLU_ALPHA = 1.702

LN_EPS = 1e-5
NORM_EPS = 1e-6
DEEP_ALPHA = (2.0 * DEPTH) ** 0.25

LANES = 128
SUBLANES = 8
MXU_DIM = 256
VMEM_LIMIT_BYTES = 56 * 1024 * 1024

DENSE_TM = 688
DENSE_TN = 512
MOE_TM = 256
MOE_UP_TN = 1024
MOE_DOWN_TN = 1024
ROUTE_BLOCK = 256

BF16 = jnp.bfloat16
F32 = jnp.float32


def _dense_kernel(x_ref, w_ref, o_ref, wbf_ref):
    @pl.when(pl.program_id(1) == 0)
    def _():
        wbf_ref[...] = w_ref[...].astype(BF16)

    o_ref[...] = jnp.dot(x_ref[...], wbf_ref[...], preferred_element_type=F32).astype(o_ref.dtype)


def _dense(x, w, *, col_block_offset=0, n_cols=None, tn=DENSE_TN, out_dtype=F32):
    m, k = x.shape
    n_cols = w.shape[1] if n_cols is None else n_cols
    tm = DENSE_TM if m % DENSE_TM == 0 else m
    assert m % tm == 0 and n_cols % tn == 0
    return pl.pallas_call(
        _dense_kernel,
        grid=(n_cols // tn, m // tm),
        in_specs=[pl.BlockSpec((tm, k), lambda j, i: (i, 0)),
                  pl.BlockSpec((k, tn), lambda j, i: (0, j + col_block_offset))],
        out_specs=pl.BlockSpec((tm, tn), lambda j, i: (i, j)),
        out_shape=jax.ShapeDtypeStruct((m, n_cols), out_dtype),
        scratch_shapes=[pltpu.VMEM((k, tn), BF16)],
        compiler_params=pltpu.CompilerParams(dimension_semantics=("arbitrary", "arbitrary"),
                                             vmem_limit_bytes=VMEM_LIMIT_BYTES),
        name="dense_proj",
    )(x, w)


def _moe_rows(n_tokens):
    n_assign = n_tokens * TOP_K
    n_tiles = -(-(n_assign + N_EXPERTS * (MOE_TM - 1)) // MOE_TM)
    return n_tiles, n_tiles * MOE_TM


def _route(logits):
    n_tokens = logits.shape[0]
    n_tiles, n_rows = _moe_rows(n_tokens)
    top_vals, top_idx = lax.top_k(logits, TOP_K)
    gates = jax.nn.softmax(top_vals, axis=-1)
    flat_e = top_idx.reshape(-1).astype(jnp.int32)
    n_assign = flat_e.shape[0]
    assert n_assign % ROUTE_BLOCK == 0
    onehot = (flat_e[:, None] == jnp.arange(N_EXPERTS, dtype=jnp.int32)[None, :]).astype(F32)
    onehot = onehot.reshape(n_assign // ROUTE_BLOCK, ROUTE_BLOCK, N_EXPERTS)
    tri = (jnp.arange(ROUTE_BLOCK)[:, None] >= jnp.arange(ROUTE_BLOCK)[None, :]).astype(F32)
    within = jnp.einsum('ij,bjk->bik', tri, onehot)
    block_total = within[:, -1, :]
    block_end = jnp.cumsum(block_total, axis=0)
    csum = (within + (block_end - block_total)[:, None, :]).reshape(n_assign, N_EXPERTS)
    rank = jnp.take_along_axis(csum, flat_e[:, None], axis=1)[:, 0].astype(jnp.int32) - 1
    counts = block_end[-1].astype(jnp.int32)
    padded = ((counts + MOE_TM - 1) // MOE_TM) * MOE_TM
    pend = jnp.cumsum(padded)
    pstart = pend - padded
    row_of = pstart[flat_e] + rank
    n_used = (pend[-1] // MOE_TM).astype(jnp.int32)
    tile_start = jnp.minimum(jnp.arange(n_tiles, dtype=jnp.int32), n_used - 1) * MOE_TM
    tile_expert = jnp.minimum(jnp.searchsorted(pend, tile_start, side="right"), N_EXPERTS - 1).astype(jnp.int32)
    order = jnp.argsort(flat_e, stable=True).astype(jnp.int32)
    rows = jnp.arange(n_rows, dtype=jnp.int32)
    e_row = tile_expert[rows // MOE_TM]
    offset = rows - pstart[e_row]
    src = order[jnp.clip((jnp.cumsum(counts) - counts)[e_row] + offset, 0, n_assign - 1)] // TOP_K
    src_tok = jnp.where(offset < counts[e_row], src, 0)
    return row_of.reshape(n_tokens, TOP_K), src_tok, gates, tile_expert, n_used.reshape(1)


def _weights_changed(te_ref, i):
    return (i == 0) | (te_ref[i] != te_ref[jnp.maximum(i - 1, 0)])


def _moe_up_kernel(te_ref, nu_ref, x_ref, w_ref, b_ref, o_ref, wbf_ref):
    i = pl.program_id(1)

    @pl.when(_weights_changed(te_ref, i))
    def _():
        wbf_ref[...] = w_ref[0, 0].astype(BF16)

    @pl.when(i < nu_ref[0])
    def _():
        h = jnp.dot(x_ref[...], wbf_ref[...], preferred_element_type=F32) + b_ref[0]
        glu = jnp.minimum(h, SWIGLU_LIMIT)
        glu = glu * jax.nn.sigmoid(SWIGLU_ALPHA * glu)
        lin = jnp.clip(h, -SWIGLU_LIMIT, SWIGLU_LIMIT) + 1.0
        tn = h.shape[1]
        prod = glu * pltpu.roll(lin, tn - 1, 1)
        lane = lax.broadcasted_iota(jnp.int32, prod.shape, 1)
        prod = jnp.where(lane % 2 == 0, prod, 0.0).astype(BF16)
        r = lax.broadcasted_iota(jnp.int32, (MXU_DIM, MXU_DIM // 2), 0)
        c = lax.broadcasted_iota(jnp.int32, (MXU_DIM, MXU_DIM // 2), 1)
        sel = (r == 2 * c).astype(BF16)
        for q in range(tn // MXU_DIM):
            part = jnp.dot(prod[:, q * MXU_DIM:(q + 1) * MXU_DIM], sel, preferred_element_type=F32)
            o_ref[:, q * (MXU_DIM // 2):(q + 1) * (MXU_DIM // 2)] = part.astype(o_ref.dtype)


def _moe_down_kernel(te_ref, nu_ref, h_ref, w_ref, b_ref, o_ref, wbf_ref):
    i = pl.program_id(1)

    @pl.when(_weights_changed(te_ref, i))
    def _():
        wbf_ref[...] = w_ref[0, 0].astype(BF16)

    @pl.when(i < nu_ref[0])
    def _():
        o_ref[...] = jnp.dot(h_ref[...], wbf_ref[...], preferred_element_type=F32) + b_ref[0]


def _moe_experts(x_rows, tile_expert, n_used, w_up, b_up, w_down, b_down, layer):
    n_rows = x_rows.shape[0]
    n_tiles = n_rows // MOE_TM
    tm = MOE_TM

    def row_tile(j, i, te, nu):
        return (jnp.minimum(i, nu[0] - 1), 0)

    tn = MOE_UP_TN
    hact = pl.pallas_call(
        _moe_up_kernel,
        grid_spec=pltpu.PrefetchScalarGridSpec(
            num_scalar_prefetch=2,
            grid=(2 * D_EXPERT // tn, n_tiles),
            in_specs=[pl.BlockSpec((tm, D_MODEL), row_tile),
                      pl.BlockSpec((1, 1, D_MODEL, tn), lambda j, i, te, nu: (layer, te[i], 0, j)),
                      pl.BlockSpec((1, 1, tn), lambda j, i, te, nu: (layer * N_EXPERTS + te[i], 0, j))],
            out_specs=pl.BlockSpec((tm, tn // 2), lambda j, i, te, nu: (jnp.minimum(i, nu[0] - 1), j)),
            scratch_shapes=[pltpu.VMEM((D_MODEL, tn), BF16)]),
        out_shape=jax.ShapeDtypeStruct((n_rows, D_EXPERT), BF16),
        compiler_params=pltpu.CompilerParams(dimension_semantics=("arbitrary", "arbitrary"),
                                             vmem_limit_bytes=VMEM_LIMIT_BYTES),
        name="moe_up_swiglu",
    )(tile_expert, n_used, x_rows, w_up, b_up.reshape(DEPTH * N_EXPERTS, 1, 2 * D_EXPERT))

    tn = MOE_DOWN_TN
    b_down = b_down.reshape(DEPTH * N_EXPERTS, 1, D_MODEL)
    return [pl.pallas_call(
        _moe_down_kernel,
        grid_spec=pltpu.PrefetchScalarGridSpec(
            num_scalar_prefetch=2,
            grid=(1, n_tiles),
            in_specs=[pl.BlockSpec((tm, D_EXPERT), row_tile),
                      pl.BlockSpec((1, 1, D_EXPERT, tn), lambda j, i, te, nu, jc=jc: (layer, te[i], 0, jc)),
                      pl.BlockSpec((1, 1, tn), lambda j, i, te, nu, jc=jc: (layer * N_EXPERTS + te[i], 0, jc))],
            out_specs=pl.BlockSpec((tm, tn), lambda j, i, te, nu: (jnp.minimum(i, nu[0] - 1), 0)),
            scratch_shapes=[pltpu.VMEM((D_EXPERT, tn), BF16)]),
        out_shape=jax.ShapeDtypeStruct((n_rows, tn), F32),
        compiler_params=pltpu.CompilerParams(dimension_semantics=("arbitrary", "arbitrary"),
                                             vmem_limit_bytes=VMEM_LIMIT_BYTES),
        name="moe_down",
    )(tile_expert, n_used, hact, w_down, b_down) for jc in range(D_MODEL // tn)]


DN_VH_STEP = 8
DN_QKH_STEP = DN_VH_STEP // (DN_V_HEADS // DN_QK_HEADS)
DN_GROUPS = DN_V_HEADS // DN_VH_STEP
DN_BASE_BLOCK = 16


def _dot(a, b):
    return jnp.dot(a, b, preferred_element_type=F32)


def _dot_nt(a, b):
    return lax.dot_general(a, b, (((1,), (1,)), ((), ())), preferred_element_type=F32)


def _split(a):
    hi = a.astype(BF16)
    return hi, (a - hi.astype(F32)).astype(BF16)


def _dot3(a, b, dot=_dot):
    return dot(a[0], b[0]) + (dot(a[0], b[1]) + dot(a[1], b[0]))


def _unit_lower_inverses(lows, ii, jj):
    eye = (ii == jj).astype(F32)
    blk16 = (ii // DN_BASE_BLOCK) == (jj // DN_BASE_BLOCK)
    blk32 = (ii // (2 * DN_BASE_BLOCK)) == (jj // (2 * DN_BASE_BLOCK))
    ps = [jnp.where(blk16, -low, 0.0) for low in lows]
    ts = [eye + p for p in ps]
    for _ in range(3):
        pss = [_split(p) for p in ps]
        ps = [_dot3(s, s) for s in pss]
        ts = [t + _dot3(_split(t), _split(p)) for t, p in zip(ts, ps)]
    for sel in (blk32 & ~blk16, ~blk32):
        tss = [_split(t) for t in ts]
        us = [_dot3(_split(jnp.where(sel, low, 0.0)), s) for low, s in zip(lows, tss)]
        ts = [t - _dot3(s, _split(u)) for t, s, u in zip(ts, tss, us)]
    return ts


def _dn_kernel(q_ref, k_ref, v_ref, z_ref, gate_ref, cq_ref, ck_ref, cv_ref, s0_ref, wq_ref, wk_ref, wv_ref,
               hp_ref, nw_ref, o_ref, s_ref, pq_ref, pk_ref, pv_ref, *, n_valid):
    c = DN_CHUNK
    hd = DN_HEAD_DIM

    @pl.when(pl.program_id(2) == 0)
    def _():
        pq_ref[c - SUBLANES:c, :] = cq_ref[0]
        pk_ref[c - SUBLANES:c, :] = ck_ref[0]
        pv_ref[c - SUBLANES:c, :] = cv_ref[0]
        s_ref[...] = s0_ref[...]

    row = lax.broadcasted_iota(jnp.int32, (c, 1), 0)
    ii = lax.broadcasted_iota(jnp.int32, (c, c), 0)
    jj = lax.broadcasted_iota(jnp.int32, (c, c), 1)

    def conv_silu(u_ref, prev_ref, w_ref):
        u = u_ref[...]
        prev = prev_ref[...]
        y = u * w_ref[DN_CONV_W - 1:DN_CONV_W, :]
        for s in range(1, DN_CONV_W):
            shifted = jnp.where(row >= s, pltpu.roll(u, s, 0), pltpu.roll(prev, s, 0))
            y = y + shifted * w_ref[DN_CONV_W - 1 - s:DN_CONV_W - s, :]
        prev_ref[...] = u
        return y * jax.nn.sigmoid(y)

    q_all = conv_silu(q_ref, pq_ref, wq_ref)
    k_all = conv_silu(k_ref, pk_ref, wk_ref)
    v_all = conv_silu(v_ref, pv_ref, wv_ref)

    gt = gate_ref[...]
    beta_all = jax.nn.sigmoid(gt)
    pre = gt + hp_ref[0, 0:1, :]
    softplus = jnp.maximum(pre, 0.0) + jnp.log1p(jnp.exp(-jnp.abs(pre)))
    g_all = -jnp.exp(hp_ref[0, 1:2, :]) * softplus
    if n_valid < c:
        live = row < n_valid
        beta_all = jnp.where(live, beta_all, 0.0)
        g_all = jnp.where(live, g_all, 0.0)
        k_all = jnp.where(live, k_all, 0.0)
        v_all = jnp.where(live, v_all, 0.0)
    gc = g_all
    s = 1
    while s < c:
        gc = gc + jnp.where(row >= s, pltpu.roll(gc, s, 0), 0.0)
        s *= 2
    gc_t = gc.T

    def l2n(t):
        return t * lax.rsqrt(jnp.sum(t * t, axis=-1, keepdims=True) + NORM_EPS)

    heads = range(DN_VH_STEP)
    rep = DN_VH_STEP // DN_QKH_STEP
    q_n = [l2n(q_all[:, j * hd:(j + 1) * hd]) * hd ** -0.5 for j in range(DN_QKH_STEP)]
    k_n = [l2n(k_all[:, j * hd:(j + 1) * hd]) for j in range(DN_QKH_STEP)]
    kk_qk = [_dot3(_split(jnp.concatenate([k, q], axis=0)), _split(k), _dot_nt) for k, q in zip(k_n, q_n)]
    g_col = [gc[:, SUBLANES + h:SUBLANES + h + 1] for h in heads]
    beta = [beta_all[:, h:h + 1] for h in heads]
    decay = [jnp.exp(jnp.minimum(g_col[h] - gc_t[SUBLANES + h:SUBLANES + h + 1, :], 0.0)) for h in heads]
    lows = [jnp.where(ii > jj, beta[h] * kk_qk[h // rep][:c] * decay[h], 0.0) for h in heads]
    t_inv = _unit_lower_inverses(lows, ii, jj)
    e_col = [jnp.exp(g) for g in g_col]
    rhs = [jnp.concatenate([v_all[:, h * hd:(h + 1) * hd] * beta[h], k_n[h // rep] * (beta[h] * e_col[h])], axis=1)
           for h in heads]
    w = [_dot3(_split(t_inv[h]), _split(rhs[h])) for h in heads]
    states = [s_ref[0, h] for h in heads]
    ks_qs = [_dot3(_split(jnp.concatenate([w[h][:, hd:], q_n[h // rep] * e_col[h]], axis=0)), _split(states[h]))
             for h in heads]
    v_new = [w[h][:, :hd] - ks_qs[h][:c] for h in heads]
    g_last = [g[c - 1:c, :] for g in g_col]
    k_tail = [k_n[h // rep] * jnp.exp(g_last[h] - g_col[h]) for h in heads]
    for h in heads:
        s_ref[0, h] = states[h] * jnp.exp(g_last[h]) + _dot3(_split(k_tail[h].T), _split(v_new[h]))
    attn = [jnp.where(ii >= jj, kk_qk[h // rep][c:] * decay[h], 0.0) for h in heads]
    outs = [ks_qs[h][c:] + _dot(attn[h].astype(BF16), v_new[h].astype(BF16)) for h in heads]
    for h in heads:
        o = outs[h]
        o = o * lax.rsqrt(jnp.mean(o * o, axis=-1, keepdims=True) + NORM_EPS) * nw_ref[...]
        z_h = z_ref[:, h * hd:(h + 1) * hd]
        o_ref[:, h * hd:(h + 1) * hd] = (o * (z_h * jax.nn.sigmoid(z_h))).astype(o_ref.dtype)


def _deltanet(proj, gates, conv_init, s0, w_conv, head_params, norm_w, *, batch, n_chunks, n_valid):
    nc = n_chunks
    qw = DN_QKH_STEP * DN_HEAD_DIM
    vw = DN_VH_STEP * DN_HEAD_DIM
    k_off = DN_KEY_DIM // qw
    v_off = 2 * DN_KEY_DIM // vw
    z_off = DN_CONV_DIM // vw
    chunk = lambda width, off: pl.BlockSpec((DN_CHUNK, width), lambda bi, g, ci: (bi * nc + ci, off + g))
    init = lambda width, off: pl.BlockSpec((1, SUBLANES, width), lambda bi, g, ci: (bi, 0, off + g))
    wspec = lambda width, off: pl.BlockSpec((DN_CONV_W, width), lambda bi, g, ci: (0, off + g))
    state = pl.BlockSpec((1, DN_VH_STEP, DN_HEAD_DIM, DN_HEAD_DIM), lambda bi, g, ci: (bi, g, 0, 0))
    return pl.pallas_call(
        functools.partial(_dn_kernel, n_valid=n_valid),
        grid=(batch, DN_GROUPS, nc),
        in_specs=[chunk(qw, 0), chunk(qw, k_off), chunk(vw, v_off), chunk(vw, z_off), chunk(LANES, 0),
                  init(qw, 0), init(qw, k_off), init(vw, v_off), state,
                  wspec(qw, 0), wspec(qw, k_off), wspec(vw, v_off),
                  pl.BlockSpec((1, SUBLANES, LANES), lambda bi, g, ci: (g, 0, 0)),
                  pl.BlockSpec((1, DN_HEAD_DIM), lambda bi, g, ci: (0, 0))],
        out_specs=[chunk(vw, 0), state],
        out_shape=[jax.ShapeDtypeStruct((batch * nc * DN_CHUNK, DN_VAL_DIM), BF16),
                   jax.ShapeDtypeStruct((batch, DN_V_HEADS, DN_HEAD_DIM, DN_HEAD_DIM), F32)],
        scratch_shapes=[pltpu.VMEM((DN_CHUNK, qw), F32), pltpu.VMEM((DN_CHUNK, qw), F32),
                        pltpu.VMEM((DN_CHUNK, vw), F32)],
        compiler_params=pltpu.CompilerParams(dimension_semantics=("arbitrary", "arbitrary", "arbitrary"),
                                             vmem_limit_bytes=VMEM_LIMIT_BYTES),
        name="deltanet_chunk",
    )(proj, proj, proj, proj, gates, conv_init, conv_init, conv_init, s0, w_conv, w_conv, w_conv,
      head_params, norm_w.reshape(1, DN_HEAD_DIM))


def _dn_gate_layout(tail):
    lead = tail.shape[:-1]
    t = tail.reshape(lead + (2, DN_GROUPS, DN_VH_STEP))
    t = jnp.swapaxes(t, -3, -2).reshape(lead + (DN_GROUPS, 2 * DN_VH_STEP))
    t = jnp.pad(t, [(0, 0)] * (len(lead) + 1) + [(0, LANES - 2 * DN_VH_STEP)])
    return t.reshape(lead + (DN_GROUPS * LANES,))


def _dn_head_params(a_log, dt_bias):
    def lay(p):
        p = p.reshape(DN_GROUPS, 1, DN_VH_STEP)
        return jnp.pad(p, ((0, 0), (0, 0), (DN_VH_STEP, LANES - 2 * DN_VH_STEP)))
    rows = jnp.concatenate([lay(dt_bias), lay(a_log)], axis=1)
    return jnp.pad(rows, ((0, 0), (0, SUBLANES - 2), (0, 0)))


ATT_BLK = 128
ATT_MASKED = -1e30


ATT_P_HEADS = (16, 8, 4)
ATT_CLASS_UNROLL = 16


def _att_prompt_kernel(slope_ref, q_ref, *refs, group, dilation, heads, has_prev):
    n_in = 4 if has_prev else 2
    kv_refs, (o_ref, lse_ref), stage = refs[:n_in], refs[n_in:n_in + 2], refs[n_in + 2:]
    hd = ATT_HEAD_DIM
    blk = ATT_BLK
    n_keys = 2 * blk if has_prev else blk
    qi = lax.broadcasted_iota(jnp.int32, (blk, n_keys), 0)
    kj = lax.broadcasted_iota(jnp.int32, (blk, n_keys), 1)
    dist = qi + (n_keys - blk) - kj
    valid = (dist >= 0) & (dist <= blk)
    if has_prev:
        valid = valid & ((kj >= blk) | (pl.program_id(1) > 0))
    dist_f = (dist * dilation).astype(F32)
    lane = lax.broadcasted_iota(jnp.int32, (blk, LANES), 1)
    head0 = pl.program_id(2) * heads

    @pl.when(pl.program_id(2) == 0)
    def _():
        lse_ref[...] = jnp.zeros(lse_ref.shape, F32)

    for h in range(heads):
        hs = slice(h * hd, (h + 1) * hd)
        slope = slope_ref[group, head0 + h]
        if dilation > 1:
            q_st, o_st = stage[0], stage[1]
            q_st[...] = q_ref[:, hs]
            for src, dst in zip(kv_refs, stage[2:]):
                dst[...] = src[:, hs]
            kv_src = stage[2:]
        else:
            kv_src = kv_refs

        def one_class(r, carry):
            if dilation > 1:
                rows = pl.ds(r, blk, stride=dilation)
                q = q_st[rows, :]
                parts = [ref[rows, :] for ref in kv_src]
            else:
                rows = pl.ds(0, blk)
                q = q_ref[:, hs]
                parts = [ref[:, hs] for ref in kv_src]
            if has_prev:
                k = jnp.concatenate([parts[1], parts[0]], axis=0)
                v = jnp.concatenate([parts[3], parts[2]], axis=0)
            else:
                k, v = parts
            sc = _dot_nt(q.astype(BF16), k.astype(BF16)) * hd ** -0.5 - slope * dist_f
            sc = jnp.where(valid, sc, ATT_MASKED)
            m = jnp.max(sc, axis=-1, keepdims=True)
            p = jnp.exp(sc - m)
            l = jnp.sum(p, axis=-1, keepdims=True)
            o = _dot(p.astype(BF16), v.astype(BF16)) / l
            if dilation > 1:
                o_st[rows, :] = o
            else:
                o_ref[:, hs] = o
            lse_ref[rows, :] = jnp.where(lane == head0 + h, m + jnp.log(l), lse_ref[rows, :])
            return carry

        if dilation > 1:
            lax.fori_loop(0, dilation, one_class, 0, unroll=ATT_CLASS_UNROLL)
            o_ref[:, hs] = o_st[...]
        else:
            one_class(0, 0)


def _att_prompt(q, kv, slopes, *, group, batch, seq):
    d = GROUP_DILATIONS[group]
    heads = ATT_P_HEADS[group]
    span = d * ATT_BLK
    assert GROUP_WINDOWS[group] // d == ATT_BLK and seq % span == 0 and HEADS_PER_GROUP % heads == 0
    ns = seq // span
    has_prev = ns > 1
    w = heads * ATT_HEAD_DIM
    n_hc = ATT_SLOT_DIM // w
    n_kv = kv.shape[1] // w
    cur = lambda col: pl.BlockSpec((span, w), lambda b, n, hc: (b * ns + n, col + hc))
    prev = lambda col: pl.BlockSpec((span, w), lambda b, n, hc: (b * ns + jnp.maximum(n - 1, 0), col + hc))
    k_col, v_col = group * n_hc, n_kv // 2 + group * n_hc
    in_specs = [pl.BlockSpec(memory_space=pltpu.SMEM), cur(group * n_hc)]
    in_specs += [cur(k_col), prev(k_col), cur(v_col), prev(v_col)] if has_prev else [cur(k_col), cur(v_col)]
    return pl.pallas_call(
        functools.partial(_att_prompt_kernel, group=group, dilation=d, heads=heads, has_prev=has_prev),
        grid=(batch, ns, n_hc),
        in_specs=in_specs,
        out_specs=[cur(0), pl.BlockSpec((span, LANES), lambda b, n, hc: (b * ns + n, 0))],
        out_shape=[jax.ShapeDtypeStruct((batch * seq, ATT_SLOT_DIM), F32),
                   jax.ShapeDtypeStruct((batch * seq, LANES), F32)],
        scratch_shapes=[pltpu.VMEM((span, ATT_HEAD_DIM), F32)] * ((4 + 2 * has_prev) if d > 1 else 0),
        compiler_params=pltpu.CompilerParams(dimension_semantics=("arbitrary", "arbitrary", "arbitrary"),
                                             vmem_limit_bytes=VMEM_LIMIT_BYTES),
        name="dilated_attention_prompt",
    )(slopes, q, *([kv] * (4 if has_prev else 2)))


ATT_S_HEADS = 4


def _att_sample_kernel(slope_ref, q_ref, kc_ref, vc_ref, kn_ref, vn_ref, o_ref, lse_ref, *, group, dilation):
    hd = ATT_HEAD_DIM
    t_new = q_ref.shape[1]
    l_buf = kc_ref.shape[1]
    window = GROUP_WINDOWS[group]
    tq = lax.broadcasted_iota(jnp.int32, (t_new, l_buf), 0)
    ic = lax.broadcasted_iota(jnp.int32, (t_new, l_buf), 1)
    dist_c = l_buf + tq - ic
    ok_c = (((ic - tq) & (dilation - 1)) == 0) & (dist_c <= window)
    tn = lax.broadcasted_iota(jnp.int32, (t_new, t_new), 0)
    un = lax.broadcasted_iota(jnp.int32, (t_new, t_new), 1)
    dist_n = tn - un
    ok_n = (dist_n >= 0) & ((dist_n & (dilation - 1)) == 0) & (dist_n <= window)
    lane = lax.broadcasted_iota(jnp.int32, (t_new, LANES), 1)
    lse_all = jnp.zeros((t_new, LANES), F32)
    for h in range(ATT_S_HEADS):
        hs = slice(h * hd, (h + 1) * hd)
        slope = slope_ref[group, pl.program_id(1) * ATT_S_HEADS + h]
        q = q_ref[0, :, hs].astype(BF16)
        sc_c = _dot_nt(q, kc_ref[0, :, hs].astype(BF16)) * hd ** -0.5 - slope * dist_c.astype(F32)
        sc_n = _dot_nt(q, kn_ref[0, :, hs].astype(BF16)) * hd ** -0.5 - slope * dist_n.astype(F32)
        sc_c = jnp.where(ok_c, sc_c, ATT_MASKED)
        sc_n = jnp.where(ok_n, sc_n, ATT_MASKED)
        m = jnp.maximum(jnp.max(sc_c, axis=-1, keepdims=True), jnp.max(sc_n, axis=-1, keepdims=True))
        p_c = jnp.exp(sc_c - m)
        p_n = jnp.exp(sc_n - m)
        l = jnp.sum(p_c, axis=-1, keepdims=True) + jnp.sum(p_n, axis=-1, keepdims=True)
        acc = (_dot(p_c.astype(BF16), vc_ref[0, :, hs].astype(BF16))
               + _dot(p_n.astype(BF16), vn_ref[0, :, hs].astype(BF16)))
        o_ref[0, :, hs] = acc / l
        lse_all = jnp.where(lane == h, m + jnp.log(l), lse_all)
    lse_ref[0, 0] = lse_all


def _att_sample(q, kv_new, cache, slopes, *, group):
    b, t, _ = q.shape
    l_buf = cache.shape[1]
    w = ATT_S_HEADS * ATT_HEAD_DIM
    per_group = ATT_SLOT_DIM // w
    n_q = q.shape[2] // w
    cache2 = cache.reshape(b, l_buf, 2 * ATT_SLOT_DIM)
    new = lambda col: pl.BlockSpec((1, t, w), lambda bi, hc: (bi, 0, col + hc))
    buf = lambda col: pl.BlockSpec((1, l_buf, w), lambda bi, hc: (bi, 0, col + hc))
    o, lse = pl.pallas_call(
        functools.partial(_att_sample_kernel, group=group, dilation=GROUP_DILATIONS[group]),
        grid=(b, per_group),
        in_specs=[pl.BlockSpec(memory_space=pltpu.SMEM), new(group * per_group), buf(0), buf(per_group),
                  new(group * per_group), new(n_q + group * per_group)],
        out_specs=[new(0), pl.BlockSpec((1, 1, t, LANES), lambda bi, hc: (bi, hc, 0, 0))],
        out_shape=[jax.ShapeDtypeStruct((b, t, ATT_SLOT_DIM), F32),
                   jax.ShapeDtypeStruct((b, per_group, t, LANES), F32)],
        compiler_params=pltpu.CompilerParams(dimension_semantics=("arbitrary", "arbitrary"),
                                             vmem_limit_bytes=VMEM_LIMIT_BYTES),
        name="dilated_attention_sample",
    )(slopes, q, cache2, cache2, kv_new, kv_new)
    lse = lse[..., :ATT_S_HEADS].transpose(0, 2, 1, 3).reshape(b * t, HEADS_PER_GROUP)
    return o.reshape(b * t, ATT_SLOT_DIM), jnp.pad(lse, ((0, 0), (0, LANES - HEADS_PER_GROUP)))


def _att_merge_kernel(o0_ref, o1_ref, o2_ref, l0_ref, l1_ref, l2_ref, out_ref):
    hd = ATT_HEAD_DIM
    lses = [l0_ref[...], l1_ref[...], l2_ref[...]]
    m = jnp.maximum(jnp.maximum(lses[0], lses[1]), lses[2])
    zs = [jnp.exp(l - m) for l in lses]
    inv = 1.0 / (zs[0] + zs[1] + zs[2])
    for h in range(HEADS_PER_GROUP):
        hs = slice(h * hd, (h + 1) * hd)
        acc = (zs[0][:, h:h + 1] * inv[:, h:h + 1]) * o0_ref[:, hs]
        acc = acc + (zs[1][:, h:h + 1] * inv[:, h:h + 1]) * o1_ref[:, hs]
        acc = acc + (zs[2][:, h:h + 1] * inv[:, h:h + 1]) * o2_ref[:, hs]
        out_ref[:, hs] = acc.astype(out_ref.dtype)


def _att_merge(outs, lses):
    n = outs[0].shape[0]
    tm = 256 if n % 256 == 0 else n
    ospec = pl.BlockSpec((tm, ATT_SLOT_DIM), lambda i: (i, 0))
    lspec = pl.BlockSpec((tm, LANES), lambda i: (i, 0))
    return pl.pallas_call(
        _att_merge_kernel,
        grid=(n // tm,),
        in_specs=[ospec] * 3 + [lspec] * 3,
        out_specs=ospec,
        out_shape=jax.ShapeDtypeStruct((n, ATT_SLOT_DIM), BF16),
        compiler_params=pltpu.CompilerParams(dimension_semantics=("arbitrary",), vmem_limit_bytes=VMEM_LIMIT_BYTES),
        name="attention_group_merge",
    )(*outs, *lses)


NORM_TM = 192


def _postnorm_kernel(x_ref, *rest, n_terms, n_col, with_router):
    if n_terms:
        gates = rest[n_terms * n_col][...]
        term = lambda k: jnp.concatenate([rest[k * n_col + j][...] for j in range(n_col)], axis=1)
        f = gates[:, 0:1] * term(0)
        for k in range(1, n_terms):
            f = f + gates[:, k:k + 1] * term(k)
        rest = rest[n_terms * n_col + 1:]
    else:
        f = rest[0][...]
        rest = rest[1:]
    g_ref, b_ref = rest[:2]
    rest = rest[2:]
    y = DEEP_ALPHA * x_ref[...] + f
    mu = jnp.mean(y, axis=-1, keepdims=True)
    yc = y - mu
    var = jnp.mean(yc * yc, axis=-1, keepdims=True)
    out = yc * lax.rsqrt(var + LN_EPS) * g_ref[...] + b_ref[...]
    if with_router:
        wr_ref, br_ref, o_ref, obf_ref, lg_ref = rest
        lg_ref[...] = jnp.dot(out, wr_ref[...], precision=lax.Precision.HIGHEST,
                              preferred_element_type=F32) + br_ref[...]
    else:
        o_ref, obf_ref = rest
    o_ref[...] = out
    obf_ref[...] = out.astype(BF16)


def _postnorm(x, f, g, b, w_router=None, b_router=None, gates=None):
    n, d = x.shape
    tm = NORM_TM if n % NORM_TM == 0 else n
    row = pl.BlockSpec((tm, d), lambda i: (i, 0))
    vec = pl.BlockSpec((1, d), lambda i: (0, 0))
    with_router = w_router is not None
    n_terms = 0 if gates is None else len(f)
    n_col = len(f[0]) if n_terms else 1
    if n_terms:
        blocks = [blk for term in f for blk in term]
        in_specs = ([row] + [pl.BlockSpec((tm, blk.shape[1]), lambda i: (i, 0)) for blk in blocks]
                    + [pl.BlockSpec((tm, n_terms), lambda i: (i, 0)), vec, vec])
        args = [x, *blocks, gates, g.reshape(1, d), b.reshape(1, d)]
    else:
        in_specs = [row, row, vec, vec]
        args = [x, f, g.reshape(1, d), b.reshape(1, d)]
    out_specs = [row, row]
    out_shape = [jax.ShapeDtypeStruct((n, d), F32), jax.ShapeDtypeStruct((n, d), BF16)]
    if with_router:
        in_specs += [pl.BlockSpec((d, N_EXPERTS), lambda i: (0, 0)), pl.BlockSpec((1, N_EXPERTS), lambda i: (0, 0))]
        args += [w_router, b_router.reshape(1, N_EXPERTS)]
        out_specs.append(pl.BlockSpec((tm, N_EXPERTS), lambda i: (i, 0)))
        out_shape.append(jax.ShapeDtypeStruct((n, N_EXPERTS), F32))
    return pl.pallas_call(
        functools.partial(_postnorm_kernel, n_terms=n_terms, n_col=n_col, with_router=with_router),
        grid=(n // tm,),
        in_specs=in_specs, out_specs=out_specs, out_shape=out_shape,
        compiler_params=pltpu.CompilerParams(dimension_semantics=("arbitrary",), vmem_limit_bytes=VMEM_LIMIT_BYTES),
        name="postnorm_router" if with_router else "postnorm",
    )(*args)


def _alibi_slopes():
    h = jnp.arange(1, N_ATT_HEADS + 1, dtype=F32)
    return (2.0 ** (-ALIBI_MAX_BIAS * h / N_ATT_HEADS)).reshape(N_GROUPS, HEADS_PER_GROUP)


def _moe_block(x_bf, logits, w_up, b_up, w_down, b_down, layer):
    row_of, src_tok, gates, tile_expert, n_used = _route(logits)
    x_rows = jnp.take(x_bf, src_tok, axis=0, mode="clip")
    y_blocks = _moe_experts(x_rows, tile_expert, n_used, w_up, b_up, w_down, b_down, layer)
    picked = [[jnp.take(y, row_of[:, k], axis=0, mode="clip") for k in range(TOP_K)] for y in y_blocks]
    return [[picked[j][k] for j in range(len(y_blocks))] for k in range(TOP_K)], gates


def kernel(x_prompt, x_sample, state_dn_S, state_dn_conv, cache_kv_w128, cache_kv_w512, cache_kv_w2048,
           w_dn_in, w_dn_conv, dn_a_log, dn_dt_bias, dn_norm_w, w_dn_out, w_kv_shared, w_att_q, w_att_out,
           ln_g, ln_b, w_router, b_router, w_up, b_up, w_down, b_down):
    bp, sp, _ = x_prompt.shape
    bs, ts, _ = x_sample.shape
    n_p, n_s = bp * sp, bs * ts
    kv_caches = (cache_kv_w128, cache_kv_w512, cache_kv_w2048)
    x = jnp.concatenate([x_prompt.reshape(n_p, D_MODEL), x_sample.reshape(n_s, D_MODEL)], axis=0)

    w_in = w_dn_in.reshape(D_MODEL, DN_IN_DIM)
    x_bf = x.astype(BF16)
    n_main = DN_CONV_DIM + DN_VAL_DIM
    proj = _dense(x_bf, w_in, n_cols=n_main)
    tail = _dense(x_bf, w_in, col_block_offset=n_main // LANES, n_cols=LANES, tn=LANES)[:, :2 * DN_V_HEADS]
    gates = _dn_gate_layout(tail)
    head_params = _dn_head_params(dn_a_log[0], dn_dt_bias[0])
    pad_s = DN_CHUNK - ts
    proj_s = proj[n_p:].reshape(bs, ts, n_main)
    o_p, s_p = _deltanet(proj, gates, jnp.zeros((bp, SUBLANES, DN_CONV_DIM), F32),
                         jnp.zeros((bp, DN_V_HEADS, DN_HEAD_DIM, DN_HEAD_DIM), F32), w_dn_conv[0], head_params,
                         dn_norm_w[0], batch=bp, n_chunks=sp // DN_CHUNK, n_valid=DN_CHUNK)
    o_s, s_s = _deltanet(jnp.pad(proj_s, ((0, 0), (0, pad_s), (0, 0))).reshape(bs * DN_CHUNK, n_main),
                         jnp.pad(gates[n_p:].reshape(bs, ts, -1), ((0, 0), (0, pad_s), (0, 0))).reshape(bs * DN_CHUNK, -1),
                         jnp.pad(state_dn_conv[0], ((0, 0), (SUBLANES - (DN_CONV_W - 1), 0), (0, 0))),
                         state_dn_S[0], w_dn_conv[0], head_params, dn_norm_w[0], batch=bs, n_chunks=1, n_valid=ts)
    cbuf_p = jnp.stack([proj[(b + 1) * sp - (DN_CONV_W - 1):(b + 1) * sp, :DN_CONV_DIM] for b in range(bp)])
    cbuf_s = proj_s[:, ts - (DN_CONV_W - 1):, :DN_CONV_DIM]
    o = jnp.concatenate([o_p, o_s.reshape(bs, DN_CHUNK, DN_VAL_DIM)[:, :ts].reshape(n_s, DN_VAL_DIM)], axis=0)
    mix = _dense(o, w_dn_out.reshape(DN_VAL_DIM, D_MODEL))
    x, x_bf, logits = _postnorm(x, mix, ln_g[0, 0], ln_b[0, 0], w_router[0], b_router[0])
    ffn, gates = _moe_block(x_bf, logits, w_up, b_up, w_down, b_down, 0)
    x, x_bf = _postnorm(x, ffn, ln_g[0, 1], ln_b[0, 1], gates=gates)

    kv = _dense(x_bf, w_kv_shared)
    def window_rows(first_row, n_seq, seq_len, g, length):
        def part(b, sel):
            r1 = first_row + (b + 1) * seq_len
            c0 = (sel * N_GROUPS + g) * ATT_SLOT_DIM
            return kv[r1 - length:r1, c0:c0 + ATT_SLOT_DIM].reshape(length, 1, HEADS_PER_GROUP, ATT_HEAD_DIM)
        return jnp.stack([jnp.concatenate([part(b, 0), part(b, 1)], axis=1) for b in range(n_seq)])

    new_kv_p = [window_rows(0, bp, sp, g, min(GROUP_WINDOWS[g], sp)) for g in range(N_GROUPS)]
    new_kv_s = [jnp.concatenate([kv_caches[g][:, ts:], window_rows(n_p, bs, ts, g, ts)], axis=1)
                for g in range(N_GROUPS)]

    slopes = _alibi_slopes()
    q = _dense(x_bf, w_att_q.reshape(D_MODEL, N_ATT_HEADS * ATT_HEAD_DIM))
    q_s = q[n_p:].reshape(bs, ts, -1)
    kv_s_rows = kv[n_p:].reshape(bs, ts, -1)
    outs_p, lses_p, outs_s, lses_s = [], [], [], []
    for g in range(N_GROUPS):
        o_g, lse_g = _att_prompt(q, kv, slopes, group=g, batch=bp, seq=sp)
        outs_p.append(o_g)
        lses_p.append(lse_g)
        o_g, lse_g = _att_sample(q_s, kv_s_rows, kv_caches[g], slopes, group=g)
        outs_s.append(o_g)
        lses_s.append(lse_g)
    att = jnp.concatenate([_att_merge(outs_p, lses_p), _att_merge(outs_s, lses_s)], axis=0)
    mix = _dense(att, w_att_out.reshape(ATT_SLOT_DIM, D_MODEL))
    x, x_bf, logits = _postnorm(x, mix, ln_g[1, 0], ln_b[1, 0], w_router[1], b_router[1])
    ffn, gates = _moe_block(x_bf, logits, w_up, b_up, w_down, b_down, 1)
    x, _ = _postnorm(x, ffn, ln_g[1, 1], ln_b[1, 1], gates=gates)

    y_prompt = x[:n_p].reshape(bp, sp, D_MODEL)
    y_sample = x[n_p:].reshape(bs, ts, D_MODEL)
    return (y_prompt, y_sample, s_p[None], cbuf_p[None], new_kv_p[0], new_kv_p[1], new_kv_p[2],
            s_s[None], cbuf_s[None], new_kv_s[0], new_kv_s[1], new_kv_s[2])
```

```python
import functools
import math

import jax
import jax.numpy as jnp
from jax import lax
from jax.experimental import pallas as pl
from jax.experimental.pallas import tpu as pltpu

D_MODEL = 4096
DEPTH = 2
N_A_LAYERS = DEPTH // 2

DN_QK_HEADS = 16
DN_V_HEADS = 32
DN_HEAD_DIM = 128
DN_KEY_DIM = DN_QK_HEADS * DN_HEAD_DIM
DN_VAL_DIM = DN_V_HEADS * DN_HEAD_DIM
DN_CONV_W = 4
DN_CONV_DIM = 2 * DN_KEY_DIM + DN_VAL_DIM
DN_IN_DIM = DN_CONV_DIM + DN_VAL_DIM + 2 * DN_V_HEADS
DN_CHUNK = 64

GROUP_WINDOWS = (128, 512, 2048)
GROUP_DILATIONS = (1, 4, 16)
N_GROUPS = 3
HEADS_PER_GROUP = 16
ATT_HEAD_DIM = 128
N_ATT_HEADS = N_GROUPS * HEADS_PER_GROUP
ATT_SLOT_DIM = HEADS_PER_GROUP * ATT_HEAD_DIM
ALIBI_MAX_BIAS = 8.0

N_EXPERTS = 32
TOP_K = 4
D_EXPERT = D_MODEL // 2
SWIGLU_LIMIT = 7.0
SWIGLU_ALPHA = 1.702

LN_EPS = 1e-5
NORM_EPS = 1e-6
DEEP_ALPHA = (2.0 * DEPTH) ** 0.25

LANES = 128
SUBLANES = 8
MXU_DIM = 256
VMEM_LIMIT_BYTES = 56 * 1024 * 1024

DENSE_TM = 688
DENSE_TN = 512
MOE_TM = 256
MOE_UP_TN = 1024
MOE_DOWN_TN = 1024
ROUTE_BLOCK = 256

BF16 = jnp.bfloat16
F32 = jnp.float32


def _dense_kernel(x_ref, w_ref, o_ref, wbf_ref):
    @pl.when(pl.program_id(1) == 0)
    def _():
        wbf_ref[...] = w_ref[...].astype(BF16)

    o_ref[...] = jnp.dot(x_ref[...], wbf_ref[...], preferred_element_type=F32).astype(o_ref.dtype)


def _dense(x, w, *, col_block_offset=0, n_cols=None, tn=DENSE_TN, out_dtype=F32):
    m, k = x.shape
    n_cols = w.shape[1] if n_cols is None else n_cols
    tm = DENSE_TM if m % DENSE_TM == 0 else m
    assert m % tm == 0 and n_cols % tn == 0
    return pl.pallas_call(
        _dense_kernel,
        grid=(n_cols // tn, m // tm),
        in_specs=[pl.BlockSpec((tm, k), lambda j, i: (i, 0)),
                  pl.BlockSpec((k, tn), lambda j, i: (0, j + col_block_offset))],
        out_specs=pl.BlockSpec((tm, tn), lambda j, i: (i, j)),
        out_shape=jax.ShapeDtypeStruct((m, n_cols), out_dtype),
        scratch_shapes=[pltpu.VMEM((k, tn), BF16)],
        compiler_params=pltpu.CompilerParams(dimension_semantics=("arbitrary", "arbitrary"),
                                             vmem_limit_bytes=VMEM_LIMIT_BYTES),
        name="dense_proj",
    )(x, w)


def _moe_rows(n_tokens):
    n_assign = n_tokens * TOP_K
    n_tiles = -(-(n_assign + N_EXPERTS * (MOE_TM - 1)) // MOE_TM)
    return n_tiles, n_tiles * MOE_TM


def _route(logits):
    n_tokens = logits.shape[0]
    n_tiles, n_rows = _moe_rows(n_tokens)
    top_vals, top_idx = lax.top_k(logits, TOP_K)
    gates = jax.nn.softmax(top_vals, axis=-1)
    flat_e = top_idx.reshape(-1).astype(jnp.int32)
    n_assign = flat_e.shape[0]
    assert n_assign % ROUTE_BLOCK == 0
    onehot = (flat_e[:, None] == jnp.arange(N_EXPERTS, dtype=jnp.int32)[None, :]).astype(F32)
    onehot = onehot.reshape(n_assign // ROUTE_BLOCK, ROUTE_BLOCK, N_EXPERTS)
    tri = (jnp.arange(ROUTE_BLOCK)[:, None] >= jnp.arange(ROUTE_BLOCK)[None, :]).astype(F32)
    within = jnp.einsum('ij,bjk->bik', tri, onehot)
    block_total = within[:, -1, :]
    block_end = jnp.cumsum(block_total, axis=0)
    csum = (within + (block_end - block_total)[:, None, :]).reshape(n_assign, N_EXPERTS)
    rank = jnp.take_along_axis(csum, flat_e[:, None], axis=1)[:, 0].astype(jnp.int32) - 1
    counts = block_end[-1].astype(jnp.int32)
    padded = ((counts + MOE_TM - 1) // MOE_TM) * MOE_TM
    pend = jnp.cumsum(padded)
    pstart = pend - padded
    row_of = pstart[flat_e] + rank
    n_used = (pend[-1] // MOE_TM).astype(jnp.int32)
    tile_start = jnp.minimum(jnp.arange(n_tiles, dtype=jnp.int32), n_used - 1) * MOE_TM
    tile_expert = jnp.minimum(jnp.searchsorted(pend, tile_start, side="right"), N_EXPERTS - 1).astype(jnp.int32)
    order = jnp.argsort(flat_e, stable=True).astype(jnp.int32)
    rows = jnp.arange(n_rows, dtype=jnp.int32)
    e_row = tile_expert[rows // MOE_TM]
    offset = rows - pstart[e_row]
    src = order[jnp.clip((jnp.cumsum(counts) - counts)[e_row] + offset, 0, n_assign - 1)] // TOP_K
    src_tok = jnp.where(offset < counts[e_row], src, 0)
    return row_of.reshape(n_tokens, TOP_K), src_tok, gates, tile_expert, n_used.reshape(1)


def _weights_changed(te_ref, i):
    return (i == 0) | (te_ref[i] != te_ref[jnp.maximum(i - 1, 0)])


def _moe_up_kernel(te_ref, nu_ref, x_ref, w_ref, b_ref, o_ref, wbf_ref):
    i = pl.program_id(1)

    @pl.when(_weights_changed(te_ref, i))
    def _():
        wbf_ref[...] = w_ref[0, 0].astype(BF16)

    @pl.when(i < nu_ref[0])
    def _():
        h = jnp.dot(x_ref[...], wbf_ref[...], preferred_element_type=F32) + b_ref[0]
        glu = jnp.minimum(h, SWIGLU_LIMIT)
        glu = glu * jax.nn.sigmoid(SWIGLU_ALPHA * glu)
        lin = jnp.clip(h, -SWIGLU_LIMIT, SWIGLU_LIMIT) + 1.0
        tn = h.shape[1]
        prod = glu * pltpu.roll(lin, tn - 1, 1)
        lane = lax.broadcasted_iota(jnp.int32, prod.shape, 1)
        prod = jnp.where(lane % 2 == 0, prod, 0.0).astype(BF16)
        r = lax.broadcasted_iota(jnp.int32, (MXU_DIM, MXU_DIM // 2), 0)
        c = lax.broadcasted_iota(jnp.int32, (MXU_DIM, MXU_DIM // 2), 1)
        sel = (r == 2 * c).astype(BF16)
        for q in range(tn // MXU_DIM):
            part = jnp.dot(prod[:, q * MXU_DIM:(q + 1) * MXU_DIM], sel, preferred_element_type=F32)
            o_ref[:, q * (MXU_DIM // 2):(q + 1) * (MXU_DIM // 2)] = part.astype(o_ref.dtype)


def _moe_down_kernel(te_ref, nu_ref, h_ref, w_ref, b_ref, o_ref, wbf_ref):
    i = pl.program_id(1)

    @pl.when(_weights_changed(te_ref, i))
    def _():
        wbf_ref[...] = w_ref[0, 0].astype(BF16)

    @pl.when(i < nu_ref[0])
    def _():
        o_ref[...] = (jnp.dot(h_ref[...], wbf_ref[...], preferred_element_type=F32) + b_ref[0]).astype(o_ref.dtype)


def _moe_experts(x_rows, tile_expert, n_used, w_up, b_up, w_down, b_down, layer):
    n_rows = x_rows.shape[0]
    n_tiles = n_rows // MOE_TM
    tm = MOE_TM

    def row_tile(j, i, te, nu):
        return (jnp.minimum(i, nu[0] - 1), 0)

    tn = MOE_UP_TN
    hact = pl.pallas_call(
        _moe_up_kernel,
        grid_spec=pltpu.PrefetchScalarGridSpec(
            num_scalar_prefetch=2,
            grid=(2 * D_EXPERT // tn, n_tiles),
            in_specs=[pl.BlockSpec((tm, D_MODEL), row_tile),
                      pl.BlockSpec((1, 1, D_MODEL, tn), lambda j, i, te, nu: (layer, te[i], 0, j)),
                      pl.BlockSpec((1, 1, tn), lambda j, i, te, nu: (layer * N_EXPERTS + te[i], 0, j))],
            out_specs=pl.BlockSpec((tm, tn // 2), lambda j, i, te, nu: (jnp.minimum(i, nu[0] - 1), j)),
            scratch_shapes=[pltpu.VMEM((D_MODEL, tn), BF16)]),
        out_shape=jax.ShapeDtypeStruct((n_rows, D_EXPERT), BF16),
        compiler_params=pltpu.CompilerParams(dimension_semantics=("arbitrary", "arbitrary"),
                                             vmem_limit_bytes=VMEM_LIMIT_BYTES),
        name="moe_up_swiglu",
    )(tile_expert, n_used, x_rows, w_up, b_up.reshape(DEPTH * N_EXPERTS, 1, 2 * D_EXPERT))

    tn = MOE_DOWN_TN
    return pl.pallas_call(
        _moe_down_kernel,
        grid_spec=pltpu.PrefetchScalarGridSpec(
            num_scalar_prefetch=2,
            grid=(D_MODEL // tn, n_tiles),
            in_specs=[pl.BlockSpec((tm, D_EXPERT), row_tile),
                      pl.BlockSpec((1, 1, D_EXPERT, tn), lambda j, i, te, nu: (layer, te[i], 0, j)),
                      pl.BlockSpec((1, 1, tn), lambda j, i, te, nu: (layer * N_EXPERTS + te[i], 0, j))],
            out_specs=pl.BlockSpec((tm, tn), lambda j, i, te, nu: (jnp.minimum(i, nu[0] - 1), j)),
            scratch_shapes=[pltpu.VMEM((D_EXPERT, tn), BF16)]),
        out_shape=jax.ShapeDtypeStruct((n_rows, D_MODEL), BF16),
        compiler_params=pltpu.CompilerParams(dimension_semantics=("arbitrary", "arbitrary"),
                                             vmem_limit_bytes=VMEM_LIMIT_BYTES),
        name="moe_down",
    )(tile_expert, n_used, hact, w_down, b_down.reshape(DEPTH * N_EXPERTS, 1, D_MODEL))


DN_VH_STEP = 8
DN_QKH_STEP = DN_VH_STEP // (DN_V_HEADS // DN_QK_HEADS)
DN_GROUPS = DN_V_HEADS // DN_VH_STEP
DN_BASE_BLOCK = 16


def _dot(a, b):
    return jnp.dot(a, b, preferred_element_type=F32)


def _dot_nt(a, b):
    return lax.dot_general(a, b, (((1,), (1,)), ((), ())), preferred_element_type=F32)


def _split(a):
    hi = a.astype(BF16)
    return hi, (a - hi.astype(F32)).astype(BF16)


def _dot3(a, b, dot=_dot):
    return dot(a[0], b[0]) + (dot(a[0], b[1]) + dot(a[1], b[0]))


def _unit_lower_inverses(lows, ii, jj):
    eye = (ii == jj).astype(F32)
    blk16 = (ii // DN_BASE_BLOCK) == (jj // DN_BASE_BLOCK)
    blk32 = (ii // (2 * DN_BASE_BLOCK)) == (jj // (2 * DN_BASE_BLOCK))
    ps = [jnp.where(blk16, -low, 0.0) for low in lows]
    ts = [eye + p for p in ps]
    for _ in range(3):
        pss = [_split(p) for p in ps]
        ps = [_dot3(s, s) for s in pss]
        ts = [t + _dot3(_split(t), _split(p)) for t, p in zip(ts, ps)]
    for sel in (blk32 & ~blk16, ~blk32):
        tss = [_split(t) for t in ts]
        us = [_dot3(_split(jnp.where(sel, low, 0.0)), s) for low, s in zip(lows, tss)]
        ts = [t - _dot3(s, _split(u)) for t, s, u in zip(ts, tss, us)]
    return ts


def _dn_kernel(q_ref, k_ref, v_ref, z_ref, gate_ref, cq_ref, ck_ref, cv_ref, s0_ref, wq_ref, wk_ref, wv_ref,
               hp_ref, nw_ref, o_ref, s_ref, pq_ref, pk_ref, pv_ref, *, n_valid):
    c = DN_CHUNK
    hd = DN_HEAD_DIM

    @pl.when(pl.program_id(2) == 0)
    def _():
        pq_ref[c - SUBLANES:c, :] = cq_ref[0]
        pk_ref[c - SUBLANES:c, :] = ck_ref[0]
        pv_ref[c - SUBLANES:c, :] = cv_ref[0]
        s_ref[...] = s0_ref[...]

    row = lax.broadcasted_iota(jnp.int32, (c, 1), 0)
    ii = lax.broadcasted_iota(jnp.int32, (c, c), 0)
    jj = lax.broadcasted_iota(jnp.int32, (c, c), 1)

    def conv_silu(u_ref, prev_ref, w_ref):
        u = u_ref[...]
        prev = prev_ref[...]
        y = u * w_ref[DN_CONV_W - 1:DN_CONV_W, :]
        for s in range(1, DN_CONV_W):
            shifted = jnp.where(row >= s, pltpu.roll(u, s, 0), pltpu.roll(prev, s, 0))
            y = y + shifted * w_ref[DN_CONV_W - 1 - s:DN_CONV_W - s, :]
        prev_ref[...] = u
        return y * jax.nn.sigmoid(y)

    q_all = conv_silu(q_ref, pq_ref, wq_ref)
    k_all = conv_silu(k_ref, pk_ref, wk_ref)
    v_all = conv_silu(v_ref, pv_ref, wv_ref)

    gt = gate_ref[...]
    beta_all = jax.nn.sigmoid(gt)
    pre = gt + hp_ref[0, 0:1, :]
    softplus = jnp.maximum(pre, 0.0) + jnp.log1p(jnp.exp(-jnp.abs(pre)))
    g_all = -jnp.exp(hp_ref[0, 1:2, :]) * softplus
    if n_valid < c:
        live = row < n_valid
        beta_all = jnp.where(live, beta_all, 0.0)
        g_all = jnp.where(live, g_all, 0.0)
        k_all = jnp.where(live, k_all, 0.0)
        v_all = jnp.where(live, v_all, 0.0)
    gc = g_all
    s = 1
    while s < c:
        gc = gc + jnp.where(row >= s, pltpu.roll(gc, s, 0), 0.0)
        s *= 2
    gc_t = gc.T

    def l2n(t):
        return t * lax.rsqrt(jnp.sum(t * t, axis=-1, keepdims=True) + NORM_EPS)

    heads = range(DN_VH_STEP)
    rep = DN_VH_STEP // DN_QKH_STEP
    q_n = [l2n(q_all[:, j * hd:(j + 1) * hd]) * hd ** -0.5 for j in range(DN_QKH_STEP)]
    k_n = [l2n(k_all[:, j * hd:(j + 1) * hd]) for j in range(DN_QKH_STEP)]
    kk_qk = [_dot3(_split(jnp.concatenate([k, q], axis=0)), _split(k), _dot_nt) for k, q in zip(k_n, q_n)]
    g_col = [gc[:, SUBLANES + h:SUBLANES + h + 1] for h in heads]
    beta = [beta_all[:, h:h + 1] for h in heads]
    decay = [jnp.exp(jnp.minimum(g_col[h] - gc_t[SUBLANES + h:SUBLANES + h + 1, :], 0.0)) for h in heads]
    lows = [jnp.where(ii > jj, beta[h] * kk_qk[h // rep][:c] * decay[h], 0.0) for h in heads]
    t_inv = _unit_lower_inverses(lows, ii, jj)
    e_col = [jnp.exp(g) for g in g_col]
    rhs = [jnp.concatenate([v_all[:, h * hd:(h + 1) * hd] * beta[h], k_n[h // rep] * (beta[h] * e_col[h])], axis=1)
           for h in heads]
    w = [_dot3(_split(t_inv[h]), _split(rhs[h])) for h in heads]
    states = [s_ref[0, h] for h in heads]
    ks_qs = [_dot3(_split(jnp.concatenate([w[h][:, hd:], q_n[h // rep] * e_col[h]], axis=0)), _split(states[h]))
             for h in heads]
    v_new = [w[h][:, :hd] - ks_qs[h][:c] for h in heads]
    g_last = [g[c - 1:c, :] for g in g_col]
    k_tail = [k_n[h // rep] * jnp.exp(g_last[h] - g_col[h]) for h in heads]
    for h in heads:
        s_ref[0, h] = states[h] * jnp.exp(g_last[h]) + _dot3(_split(k_tail[h].T), _split(v_new[h]))
    attn = [jnp.where(ii >= jj, kk_qk[h // rep][c:] * decay[h], 0.0) for h in heads]
    outs = [ks_qs[h][c:] + _dot(attn[h].astype(BF16), v_new[h].astype(BF16)) for h in heads]
    for h in heads:
        o = outs[h]
        o = o * lax.rsqrt(jnp.mean(o * o, axis=-1, keepdims=True) + NORM_EPS) * nw_ref[...]
        z_h = z_ref[:, h * hd:(h + 1) * hd]
        o_ref[:, h * hd:(h + 1) * hd] = (o * (z_h * jax.nn.sigmoid(z_h))).astype(o_ref.dtype)


def _deltanet(proj, gates, conv_init, s0, w_conv, head_params, norm_w, *, batch, n_chunks, n_valid):
    nc = n_chunks
    qw = DN_QKH_STEP * DN_HEAD_DIM
    vw = DN_VH_STEP * DN_HEAD_DIM
    k_off = DN_KEY_DIM // qw
    v_off = 2 * DN_KEY_DIM // vw
    z_off = DN_CONV_DIM // vw
    chunk = lambda width, off: pl.BlockSpec((DN_CHUNK, width), lambda bi, g, ci: (bi * nc + ci, off + g))
    init = lambda width, off: pl.BlockSpec((1, SUBLANES, width), lambda bi, g, ci: (bi, 0, off + g))
    wspec = lambda width, off: pl.BlockSpec((DN_CONV_W, width), lambda bi, g, ci: (0, off + g))
    state = pl.BlockSpec((1, DN_VH_STEP, DN_HEAD_DIM, DN_HEAD_DIM), lambda bi, g, ci: (bi, g, 0, 0))
    return pl.pallas_call(
        functools.partial(_dn_kernel, n_valid=n_valid),
        grid=(batch, DN_GROUPS, nc),
        in_specs=[chunk(qw, 0), chunk(qw, k_off), chunk(vw, v_off), chunk(vw, z_off), chunk(LANES, 0),
                  init(qw, 0), init(qw, k_off), init(vw, v_off), state,
                  wspec(qw, 0), wspec(qw, k_off), wspec(vw, v_off),
                  pl.BlockSpec((1, SUBLANES, LANES), lambda bi, g, ci: (g, 0, 0)),
                  pl.BlockSpec((1, DN_HEAD_DIM), lambda bi, g, ci: (0, 0))],
        out_specs=[chunk(vw, 0), state],
        out_shape=[jax.ShapeDtypeStruct((batch * nc * DN_CHUNK, DN_VAL_DIM), BF16),
                   jax.ShapeDtypeStruct((batch, DN_V_HEADS, DN_HEAD_DIM, DN_HEAD_DIM), F32)],
        scratch_shapes=[pltpu.VMEM((DN_CHUNK, qw), F32), pltpu.VMEM((DN_CHUNK, qw), F32),
                        pltpu.VMEM((DN_CHUNK, vw), F32)],
        compiler_params=pltpu.CompilerParams(dimension_semantics=("arbitrary", "arbitrary", "arbitrary"),
                                             vmem_limit_bytes=VMEM_LIMIT_BYTES),
        name="deltanet_chunk",
    )(proj, proj, proj, proj, gates, conv_init, conv_init, conv_init, s0, w_conv, w_conv, w_conv,
      head_params, norm_w.reshape(1, DN_HEAD_DIM))


def _dn_gate_layout(tail):
    lead = tail.shape[:-1]
    t = tail.reshape(lead + (2, DN_GROUPS, DN_VH_STEP))
    t = jnp.swapaxes(t, -3, -2).reshape(lead + (DN_GROUPS, 2 * DN_VH_STEP))
    t = jnp.pad(t, [(0, 0)] * (len(lead) + 1) + [(0, LANES - 2 * DN_VH_STEP)])
    return t.reshape(lead + (DN_GROUPS * LANES,))


def _dn_head_params(a_log, dt_bias):
    def lay(p):
        p = p.reshape(DN_GROUPS, 1, DN_VH_STEP)
        return jnp.pad(p, ((0, 0), (0, 0), (DN_VH_STEP, LANES - 2 * DN_VH_STEP)))
    rows = jnp.concatenate([lay(dt_bias), lay(a_log)], axis=1)
    return jnp.pad(rows, ((0, 0), (0, SUBLANES - 2), (0, 0)))


ATT_BLK = 128
ATT_MASKED = -1e30


ATT_P_HEADS = (16, 8, 4)
ATT_CLASS_UNROLL = 16


def _att_prompt_kernel(slope_ref, q_ref, *refs, group, dilation, heads, has_prev):
    n_in = 4 if has_prev else 2
    kv_refs, (o_ref, lse_ref), stage = refs[:n_in], refs[n_in:n_in + 2], refs[n_in + 2:]
    hd = ATT_HEAD_DIM
    blk = ATT_BLK
    n_keys = 2 * blk if has_prev else blk
    qi = lax.broadcasted_iota(jnp.int32, (blk, n_keys), 0)
    kj = lax.broadcasted_iota(jnp.int32, (blk, n_keys), 1)
    dist = qi + (n_keys - blk) - kj
    valid = (dist >= 0) & (dist <= blk)
    if has_prev:
        valid = valid & ((kj >= blk) | (pl.program_id(1) > 0))
    dist_f = (dist * dilation).astype(F32)
    lane = lax.broadcasted_iota(jnp.int32, (blk, LANES), 1)
    head0 = pl.program_id(2) * heads

    @pl.when(pl.program_id(2) == 0)
    def _():
        lse_ref[...] = jnp.zeros(lse_ref.shape, F32)

    for h in range(heads):
        hs = slice(h * hd, (h + 1) * hd)
        slope = slope_ref[group, head0 + h]
        if dilation > 1:
            q_st, o_st = stage[0], stage[1]
            q_st[...] = q_ref[:, hs]
            for src, dst in zip(kv_refs, stage[2:]):
                dst[...] = src[:, hs]
            kv_src = stage[2:]
        else:
            kv_src = kv_refs

        def one_class(r, carry):
            if dilation > 1:
                rows = pl.ds(r, blk, stride=dilation)
                q = q_st[rows, :]
                parts = [ref[rows, :] for ref in kv_src]
            else:
                rows = pl.ds(0, blk)
                q = q_ref[:, hs]
                parts = [ref[:, hs] for ref in kv_src]
            if has_prev:
                k = jnp.concatenate([parts[1], parts[0]], axis=0)
                v = jnp.concatenate([parts[3], parts[2]], axis=0)
            else:
                k, v = parts
            sc = _dot_nt(q.astype(BF16), k.astype(BF16)) * hd ** -0.5 - slope * dist_f
            sc = jnp.where(valid, sc, ATT_MASKED)
            m = jnp.max(sc, axis=-1, keepdims=True)
            p = jnp.exp(sc - m)
            l = jnp.sum(p, axis=-1, keepdims=True)
            o = _dot(p.astype(BF16), v.astype(BF16)) / l
            if dilation > 1:
                o_st[rows, :] = o
            else:
                o_ref[:, hs] = o
            lse_ref[rows, :] = jnp.where(lane == head0 + h, m + jnp.log(l), lse_ref[rows, :])
            return carry

        if dilation > 1:
            lax.fori_loop(0, dilation, one_class, 0, unroll=ATT_CLASS_UNROLL)
            o_ref[:, hs] = o_st[...]
        else:
            one_class(0, 0)


def _att_prompt(q, kv, slopes, *, group, batch, seq):
    d = GROUP_DILATIONS[group]
    heads = ATT_P_HEADS[group]
    span = d * ATT_BLK
    assert GROUP_WINDOWS[group] // d == ATT_BLK and seq % span == 0 and HEADS_PER_GROUP % heads == 0
    ns = seq // span
    has_prev = ns > 1
    w = heads * ATT_HEAD_DIM
    n_hc = ATT_SLOT_DIM // w
    n_kv = kv.shape[1] // w
    cur = lambda col: pl.BlockSpec((span, w), lambda b, n, hc: (b * ns + n, col + hc))
    prev = lambda col: pl.BlockSpec((span, w), lambda b, n, hc: (b * ns + jnp.maximum(n - 1, 0), col + hc))
    k_col, v_col = group * n_hc, n_kv // 2 + group * n_hc
    in_specs = [pl.BlockSpec(memory_space=pltpu.SMEM), cur(group * n_hc)]
    in_specs += [cur(k_col), prev(k_col), cur(v_col), prev(v_col)] if has_prev else [cur(k_col), cur(v_col)]
    return pl.pallas_call(
        functools.partial(_att_prompt_kernel, group=group, dilation=d, heads=heads, has_prev=has_prev),
        grid=(batch, ns, n_hc),
        in_specs=in_specs,
        out_specs=[cur(0), pl.BlockSpec((span, LANES), lambda b, n, hc: (b * ns + n, 0))],
        out_shape=[jax.ShapeDtypeStruct((batch * seq, ATT_SLOT_DIM), F32),
                   jax.ShapeDtypeStruct((batch * seq, LANES), F32)],
        scratch_shapes=[pltpu.VMEM((span, ATT_HEAD_DIM), F32)] * ((4 + 2 * has_prev) if d > 1 else 0),
        compiler_params=pltpu.CompilerParams(dimension_semantics=("arbitrary", "arbitrary", "arbitrary"),
                                             vmem_limit_bytes=VMEM_LIMIT_BYTES),
        name="dilated_attention_prompt",
    )(slopes, q, *([kv] * (4 if has_prev else 2)))


ATT_S_HEADS = 4


def _att_sample_kernel(slope_ref, q_ref, kc_ref, vc_ref, kn_ref, vn_ref, o_ref, lse_ref, *, group, dilation):
    hd = ATT_HEAD_DIM
    t_new = q_ref.shape[1]
    l_buf = kc_ref.shape[1]
    window = GROUP_WINDOWS[group]
    tq = lax.broadcasted_iota(jnp.int32, (t_new, l_buf), 0)
    ic = lax.broadcasted_iota(jnp.int32, (t_new, l_buf), 1)
    dist_c = l_buf + tq - ic
    ok_c = (((ic - tq) & (dilation - 1)) == 0) & (dist_c <= window)
    tn = lax.broadcasted_iota(jnp.int32, (t_new, t_new), 0)
    un = lax.broadcasted_iota(jnp.int32, (t_new, t_new), 1)
    dist_n = tn - un
    ok_n = (dist_n >= 0) & ((dist_n & (dilation - 1)) == 0) & (dist_n <= window)
    lane = lax.broadcasted_iota(jnp.int32, (t_new, LANES), 1)
    lse_all = jnp.zeros((t_new, LANES), F32)
    for h in range(ATT_S_HEADS):
        hs = slice(h * hd, (h + 1) * hd)
        slope = slope_ref[group, pl.program_id(1) * ATT_S_HEADS + h]
        q = q_ref[0, :, hs].astype(BF16)
        sc_c = _dot_nt(q, kc_ref[0, :, hs].astype(BF16)) * hd ** -0.5 - slope * dist_c.astype(F32)
        sc_n = _dot_nt(q, kn_ref[0, :, hs].astype(BF16)) * hd ** -0.5 - slope * dist_n.astype(F32)
        sc_c = jnp.where(ok_c, sc_c, ATT_MASKED)
        sc_n = jnp.where(ok_n, sc_n, ATT_MASKED)
        m = jnp.maximum(jnp.max(sc_c, axis=-1, keepdims=True), jnp.max(sc_n, axis=-1, keepdims=True))
        p_c = jnp.exp(sc_c - m)
        p_n = jnp.exp(sc_n - m)
        l = jnp.sum(p_c, axis=-1, keepdims=True) + jnp.sum(p_n, axis=-1, keepdims=True)
        acc = (_dot(p_c.astype(BF16), vc_ref[0, :, hs].astype(BF16))
               + _dot(p_n.astype(BF16), vn_ref[0, :, hs].astype(BF16)))
        o_ref[0, :, hs] = acc / l
        lse_all = jnp.where(lane == h, m + jnp.log(l), lse_all)
    lse_ref[0, 0] = lse_all


def _att_sample(q, kv_new, cache, slopes, *, group):
    b, t, _ = q.shape
    l_buf = cache.shape[1]
    w = ATT_S_HEADS * ATT_HEAD_DIM
    per_group = ATT_SLOT_DIM // w
    n_q = q.shape[2] // w
    cache2 = cache.reshape(b, l_buf, 2 * ATT_SLOT_DIM)
    new = lambda col: pl.BlockSpec((1, t, w), lambda bi, hc: (bi, 0, col + hc))
    buf = lambda col: pl.BlockSpec((1, l_buf, w), lambda bi, hc: (bi, 0, col + hc))
    o, lse = pl.pallas_call(
        functools.partial(_att_sample_kernel, group=group, dilation=GROUP_DILATIONS[group]),
        grid=(b, per_group),
        in_specs=[pl.BlockSpec(memory_space=pltpu.SMEM), new(group * per_group), buf(0), buf(per_group),
                  new(group * per_group), new(n_q + group * per_group)],
        out_specs=[new(0), pl.BlockSpec((1, 1, t, LANES), lambda bi, hc: (bi, hc, 0, 0))],
        out_shape=[jax.ShapeDtypeStruct((b, t, ATT_SLOT_DIM), F32),
                   jax.ShapeDtypeStruct((b, per_group, t, LANES), F32)],
        compiler_params=pltpu.CompilerParams(dimension_semantics=("arbitrary", "arbitrary"),
                                             vmem_limit_bytes=VMEM_LIMIT_BYTES),
        name="dilated_attention_sample",
    )(slopes, q, cache2, cache2, kv_new, kv_new)
    lse = lse[..., :ATT_S_HEADS].transpose(0, 2, 1, 3).reshape(b * t, HEADS_PER_GROUP)
    return o.reshape(b * t, ATT_SLOT_DIM), jnp.pad(lse, ((0, 0), (0, LANES - HEADS_PER_GROUP)))


def _att_merge_kernel(o0_ref, o1_ref, o2_ref, l0_ref, l1_ref, l2_ref, out_ref):
    hd = ATT_HEAD_DIM
    lses = [l0_ref[...], l1_ref[...], l2_ref[...]]
    m = jnp.maximum(jnp.maximum(lses[0], lses[1]), lses[2])
    zs = [jnp.exp(l - m) for l in lses]
    inv = 1.0 / (zs[0] + zs[1] + zs[2])
    for h in range(HEADS_PER_GROUP):
        hs = slice(h * hd, (h + 1) * hd)
        acc = (zs[0][:, h:h + 1] * inv[:, h:h + 1]) * o0_ref[:, hs]
        acc = acc + (zs[1][:, h:h + 1] * inv[:, h:h + 1]) * o1_ref[:, hs]
        acc = acc + (zs[2][:, h:h + 1] * inv[:, h:h + 1]) * o2_ref[:, hs]
        out_ref[:, hs] = acc.astype(out_ref.dtype)


def _att_merge(outs, lses):
    n = outs[0].shape[0]
    tm = 256 if n % 256 == 0 else n
    ospec = pl.BlockSpec((tm, ATT_SLOT_DIM), lambda i: (i, 0))
    lspec = pl.BlockSpec((tm, LANES), lambda i: (i, 0))
    return pl.pallas_call(
        _att_merge_kernel,
        grid=(n // tm,),
        in_specs=[ospec] * 3 + [lspec] * 3,
        out_specs=ospec,
        out_shape=jax.ShapeDtypeStruct((n, ATT_SLOT_DIM), BF16),
        compiler_params=pltpu.CompilerParams(dimension_semantics=("arbitrary",), vmem_limit_bytes=VMEM_LIMIT_BYTES),
        name="attention_group_merge",
    )(*outs, *lses)


NORM_TM = 192


def _postnorm_kernel(x_ref, *rest, n_terms, with_router):
    if n_terms:
        gates = rest[n_terms][...]
        f = gates[:, 0:1] * rest[0][...].astype(F32)
        for k in range(1, n_terms):
            f = f + gates[:, k:k + 1] * rest[k][...].astype(F32)
        rest = rest[n_terms + 1:]
    else:
        f = rest[0][...]
        rest = rest[1:]
    g_ref, b_ref = rest[:2]
    rest = rest[2:]
    y = DEEP_ALPHA * x_ref[...] + f
    mu = jnp.mean(y, axis=-1, keepdims=True)
    yc = y - mu
    var = jnp.mean(yc * yc, axis=-1, keepdims=True)
    out = yc * lax.rsqrt(var + LN_EPS) * g_ref[...] + b_ref[...]
    if with_router:
        wr_ref, br_ref, o_ref, obf_ref, lg_ref = rest
        lg_ref[...] = jnp.dot(out, wr_ref[...], precision=lax.Precision.HIGHEST,
                              preferred_element_type=F32) + br_ref[...]
    else:
        o_ref, obf_ref = rest
    o_ref[...] = out
    obf_ref[...] = out.astype(BF16)


def _postnorm(x, f, g, b, w_router=None, b_router=None, gates=None):
    n, d = x.shape
    tm = NORM_TM if n % NORM_TM == 0 else n
    row = pl.BlockSpec((tm, d), lambda i: (i, 0))
    vec = pl.BlockSpec((1, d), lambda i: (0, 0))
    with_router = w_router is not None
    n_terms = 0 if gates is None else len(f)
    if n_terms:
        in_specs = [row] * (1 + n_terms) + [pl.BlockSpec((tm, n_terms), lambda i: (i, 0)), vec, vec]
        args = [x, *f, gates, g.reshape(1, d), b.reshape(1, d)]
    else:
        in_specs = [row, row, vec, vec]
        args = [x, f, g.reshape(1, d), b.reshape(1, d)]
    out_specs = [row, row]
    out_shape = [jax.ShapeDtypeStruct((n, d), F32), jax.ShapeDtypeStruct((n, d), BF16)]
    if with_router:
        in_specs += [pl.BlockSpec((d, N_EXPERTS), lambda i: (0, 0)), pl.BlockSpec((1, N_EXPERTS), lambda i: (0, 0))]
        args += [w_router, b_router.reshape(1, N_EXPERTS)]
        out_specs.append(pl.BlockSpec((tm, N_EXPERTS), lambda i: (i, 0)))
        out_shape.append(jax.ShapeDtypeStruct((n, N_EXPERTS), F32))
    return pl.pallas_call(
        functools.partial(_postnorm_kernel, n_terms=n_terms, with_router=with_router),
        grid=(n // tm,),
        in_specs=in_specs, out_specs=out_specs, out_shape=out_shape,
        compiler_params=pltpu.CompilerParams(dimension_semantics=("arbitrary",), vmem_limit_bytes=VMEM_LIMIT_BYTES),
        name="postnorm_router" if with_router else "postnorm",
    )(*args)


def _alibi_slopes():
    h = jnp.arange(1, N_ATT_HEADS + 1, dtype=F32)
    return (2.0 ** (-ALIBI_MAX_BIAS * h / N_ATT_HEADS)).reshape(N_GROUPS, HEADS_PER_GROUP)


def _moe_block(x_bf, logits, w_up, b_up, w_down, b_down, layer):
    row_of, src_tok, gates, tile_expert, n_used = _route(logits)
    x_rows = jnp.take(x_bf, src_tok, axis=0, mode="clip")
    y_rows = _moe_experts(x_rows, tile_expert, n_used, w_up, b_up, w_down, b_down, layer)
    return [jnp.take(y_rows, row_of[:, k], axis=0, mode="clip") for k in range(TOP_K)], gates


def kernel(x_prompt, x_sample, state_dn_S, state_dn_conv, cache_kv_w128, cache_kv_w512, cache_kv_w2048,
           w_dn_in, w_dn_conv, dn_a_log, dn_dt_bias, dn_norm_w, w_dn_out, w_kv_shared, w_att_q, w_att_out,
           ln_g, ln_b, w_router, b_router, w_up, b_up, w_down, b_down):
    bp, sp, _ = x_prompt.shape
    bs, ts, _ = x_sample.shape
    n_p, n_s = bp * sp, bs * ts
    kv_caches = (cache_kv_w128, cache_kv_w512, cache_kv_w2048)
    x = jnp.concatenate([x_prompt.reshape(n_p, D_MODEL), x_sample.reshape(n_s, D_MODEL)], axis=0)

    w_in = w_dn_in.reshape(D_MODEL, DN_IN_DIM)
    x_bf = x.astype(BF16)
    n_main = DN_CONV_DIM + DN_VAL_DIM
    proj = _dense(x_bf, w_in, n_cols=n_main)
    tail = _dense(x_bf, w_in, col_block_offset=n_main // LANES, n_cols=LANES, tn=LANES)[:, :2 * DN_V_HEADS]
    gates = _dn_gate_layout(tail)
    head_params = _dn_head_params(dn_a_log[0], dn_dt_bias[0])
    pad_s = DN_CHUNK - ts
    proj_s = proj[n_p:].reshape(bs, ts, n_main)
    o_p, s_p = _deltanet(proj, gates, jnp.zeros((bp, SUBLANES, DN_CONV_DIM), F32),
                         jnp.zeros((bp, DN_V_HEADS, DN_HEAD_DIM, DN_HEAD_DIM), F32), w_dn_conv[0], head_params,
                         dn_norm_w[0], batch=bp, n_chunks=sp // DN_CHUNK, n_valid=DN_CHUNK)
    o_s, s_s = _deltanet(jnp.pad(proj_s, ((0, 0), (0, pad_s), (0, 0))).reshape(bs * DN_CHUNK, n_main),
                         jnp.pad(gates[n_p:].reshape(bs, ts, -1), ((0, 0), (0, pad_s), (0, 0))).reshape(bs * DN_CHUNK, -1),
                         jnp.pad(state_dn_conv[0], ((0, 0), (SUBLANES - (DN_CONV_W - 1), 0), (0, 0))),
                         state_dn_S[0], w_dn_conv[0], head_params, dn_norm_w[0], batch=bs, n_chunks=1, n_valid=ts)
    cbuf_p = jnp.stack([proj[(b + 1) * sp - (DN_CONV_W - 1):(b + 1) * sp, :DN_CONV_DIM] for b in range(bp)])
    cbuf_s = proj_s[:, ts - (DN_CONV_W - 1):, :DN_CONV_DIM]
    o = jnp.concatenate([o_p, o_s.reshape(bs, DN_CHUNK, DN_VAL_DIM)[:, :ts].reshape(n_s, DN_VAL_DIM)], axis=0)
    mix = _dense(o, w_dn_out.reshape(DN_VAL_DIM, D_MODEL))
    x, x_bf, logits = _postnorm(x, mix, ln_g[0, 0], ln_b[0, 0], w_router[0], b_router[0])
    ffn, gates = _moe_block(x_bf, logits, w_up, b_up, w_down, b_down, 0)
    x, x_bf = _postnorm(x, ffn, ln_g[0, 1], ln_b[0, 1], gates=gates)

    kv = _dense(x_bf, w_kv_shared)
    def window_rows(first_row, n_seq, seq_len, g, length):
        def part(b, sel):
            r1 = first_row + (b + 1) * seq_len
            c0 = (sel * N_GROUPS + g) * ATT_SLOT_DIM
            return kv[r1 - length:r1, c0:c0 + ATT_SLOT_DIM].reshape(length, 1, HEADS_PER_GROUP, ATT_HEAD_DIM)
        return jnp.stack([jnp.concatenate([part(b, 0), part(b, 1)], axis=1) for b in range(n_seq)])

    new_kv_p = [window_rows(0, bp, sp, g, min(GROUP_WINDOWS[g], sp)) for g in range(N_GROUPS)]
    new_kv_s = [jnp.concatenate([kv_caches[g][:, ts:], window_rows(n_p, bs, ts, g, ts)], axis=1)
                for g in range(N_GROUPS)]

    slopes = _alibi_slopes()
    q = _dense(x_bf, w_att_q.reshape(D_MODEL, N_ATT_HEADS * ATT_HEAD_DIM))
    q_s = q[n_p:].reshape(bs, ts, -1)
    kv_s_rows = kv[n_p:].reshape(bs, ts, -1)
    outs_p, lses_p, outs_s, lses_s = [], [], [], []
    for g in range(N_GROUPS):
        o_g, lse_g = _att_prompt(q, kv, slopes, group=g, batch=bp, seq=sp)
        outs_p.append(o_g)
        lses_p.append(lse_g)
        o_g, lse_g = _att_sample(q_s, kv_s_rows, kv_caches[g], slopes, group=g)
        outs_s.append(o_g)
        lses_s.append(lse_g)
    att = jnp.concatenate([_att_merge(outs_p, lses_p), _att_merge(outs_s, lses_s)], axis=0)
    mix = _dense(att, w_att_out.reshape(ATT_SLOT_DIM, D_MODEL))
    x, x_bf, logits = _postnorm(x, mix, ln_g[1, 0], ln_b[1, 0], w_router[1], b_router[1])
    ffn, gates = _moe_block(x_bf, logits, w_up, b_up, w_down, b_down, 1)
    x, _ = _postnorm(x, ffn, ln_g[1, 1], ln_b[1, 1], gates=gates)

    y_prompt = x[:n_p].reshape(bp, sp, D_MODEL)
    y_sample = x[n_p:].reshape(bs, ts, D_MODEL)
    return (y_prompt, y_sample, s_p[None], cbuf_p[None], new_kv_p[0], new_kv_p[1], new_kv_p[2],
            s_s[None], cbuf_s[None], new_kv_s[0], new_kv_s[1], new_kv_s[2])
```

```python
import functools
import math

import jax
import jax.numpy as jnp
from jax import lax
from jax.experimental import pallas as pl
from jax.experimental.pallas import tpu as pltpu

D_MODEL = 4096
DEPTH = 2
N_A_LAYERS = DEPTH // 2

DN_QK_HEADS = 16
DN_V_HEADS = 32
DN_HEAD_DIM = 128
DN_KEY_DIM = DN_QK_HEADS * DN_HEAD_DIM
DN_VAL_DIM = DN_V_HEADS * DN_HEAD_DIM
DN_CONV_W = 4
DN_CONV_DIM = 2 * DN_KEY_DIM + DN_VAL_DIM
DN_IN_DIM = DN_CONV_DIM + DN_VAL_DIM + 2 * DN_V_HEADS
DN_CHUNK = 64

GROUP_WINDOWS = (128, 512, 2048)
GROUP_DILATIONS = (1, 4, 16)
N_GROUPS = 3
HEADS_PER_GROUP = 16
ATT_HEAD_DIM = 128
N_ATT_HEADS = N_GROUPS * HEADS_PER_GROUP
ATT_SLOT_DIM = HEADS_PER_GROUP * ATT_HEAD_DIM
ALIBI_MAX_BIAS = 8.0

N_EXPERTS = 32
TOP_K = 4
D_EXPERT = D_MODEL // 2
SWIGLU_LIMIT = 7.0
SWIGLU_ALPHA = 1.702

LN_EPS = 1e-5
NORM_EPS = 1e-6
DEEP_ALPHA = (2.0 * DEPTH) ** 0.25

LANES = 128
SUBLANES = 8
MXU_DIM = 256
VMEM_LIMIT_BYTES = 56 * 1024 * 1024

DENSE_TM = 688
DENSE_TN = 512
MOE_TM = 256
MOE_UP_TN = 1024
MOE_DOWN_TN = 1024
ROUTE_BLOCK = 256

BF16 = jnp.bfloat16
F32 = jnp.float32


def _dense_kernel(x_ref, w_ref, o_ref, wbf_ref):
    @pl.when(pl.program_id(1) == 0)
    def _():
        wbf_ref[...] = w_ref[...].astype(BF16)

    o_ref[...] = jnp.dot(x_ref[...], wbf_ref[...], preferred_element_type=F32).astype(o_ref.dtype)


def _dense(x, w, *, col_block_offset=0, n_cols=None, tn=DENSE_TN, out_dtype=F32):
    m, k = x.shape
    n_cols = w.shape[1] if n_cols is None else n_cols
    tm = DENSE_TM if m % DENSE_TM == 0 else m
    assert m % tm == 0 and n_cols % tn == 0
    return pl.pallas_call(
        _dense_kernel,
        grid=(n_cols // tn, m // tm),
        in_specs=[pl.BlockSpec((tm, k), lambda j, i: (i, 0)),
                  pl.BlockSpec((k, tn), lambda j, i: (0, j + col_block_offset))],
        out_specs=pl.BlockSpec((tm, tn), lambda j, i: (i, j)),
        out_shape=jax.ShapeDtypeStruct((m, n_cols), out_dtype),
        scratch_shapes=[pltpu.VMEM((k, tn), BF16)],
        compiler_params=pltpu.CompilerParams(dimension_semantics=("arbitrary", "arbitrary"),
                                             vmem_limit_bytes=VMEM_LIMIT_BYTES),
        name="dense_proj",
    )(x, w)


def _moe_rows(n_tokens):
    n_assign = n_tokens * TOP_K
    n_tiles = -(-(n_assign + N_EXPERTS * (MOE_TM - 1)) // MOE_TM)
    return n_tiles, n_tiles * MOE_TM


def _route(logits):
    n_tokens = logits.shape[0]
    n_tiles, n_rows = _moe_rows(n_tokens)
    top_vals, top_idx = lax.top_k(logits, TOP_K)
    gates = jax.nn.softmax(top_vals, axis=-1)
    flat_e = top_idx.reshape(-1).astype(jnp.int32)
    n_assign = flat_e.shape[0]
    assert n_assign % ROUTE_BLOCK == 0
    onehot = (flat_e[:, None] == jnp.arange(N_EXPERTS, dtype=jnp.int32)[None, :]).astype(F32)
    onehot = onehot.reshape(n_assign // ROUTE_BLOCK, ROUTE_BLOCK, N_EXPERTS)
    tri = (jnp.arange(ROUTE_BLOCK)[:, None] >= jnp.arange(ROUTE_BLOCK)[None, :]).astype(F32)
    within = jnp.einsum('ij,bjk->bik', tri, onehot)
    block_total = within[:, -1, :]
    block_end = jnp.cumsum(block_total, axis=0)
    csum = (within + (block_end - block_total)[:, None, :]).reshape(n_assign, N_EXPERTS)
    rank = jnp.take_along_axis(csum, flat_e[:, None], axis=1)[:, 0].astype(jnp.int32) - 1
    counts = block_end[-1].astype(jnp.int32)
    padded = ((counts + MOE_TM - 1) // MOE_TM) * MOE_TM
    pend = jnp.cumsum(padded)
    pstart = pend - padded
    row_of = pstart[flat_e] + rank
    n_used = (pend[-1] // MOE_TM).astype(jnp.int32)
    tile_start = jnp.minimum(jnp.arange(n_tiles, dtype=jnp.int32), n_used - 1) * MOE_TM
    tile_expert = jnp.minimum(jnp.searchsorted(pend, tile_start, side="right"), N_EXPERTS - 1).astype(jnp.int32)
    order = jnp.argsort(flat_e, stable=True).astype(jnp.int32)
    rows = jnp.arange(n_rows, dtype=jnp.int32)
    e_row = tile_expert[rows // MOE_TM]
    offset = rows - pstart[e_row]
    src = order[jnp.clip((jnp.cumsum(counts) - counts)[e_row] + offset, 0, n_assign - 1)] // TOP_K
    src_tok = jnp.where(offset < counts[e_row], src, 0)
    return row_of.reshape(n_tokens, TOP_K), src_tok, gates, tile_expert, n_used.reshape(1)


def _weights_changed(te_ref, i):
    return (i == 0) | (te_ref[i] != te_ref[jnp.maximum(i - 1, 0)])


def _moe_up_kernel(te_ref, nu_ref, x_ref, w_ref, b_ref, o_ref, wbf_ref):
    i = pl.program_id(1)

    @pl.when(_weights_changed(te_ref, i))
    def _():
        wbf_ref[...] = w_ref[0, 0].astype(BF16)

    @pl.when(i < nu_ref[0])
    def _():
        h = jnp.dot(x_ref[...], wbf_ref[...], preferred_element_type=F32) + b_ref[0]
        glu = jnp.minimum(h, SWIGLU_LIMIT)
        glu = glu * jax.nn.sigmoid(SWIGLU_ALPHA * glu)
        lin = jnp.clip(h, -SWIGLU_LIMIT, SWIGLU_LIMIT) + 1.0
        tn = h.shape[1]
        prod = glu * pltpu.roll(lin, tn - 1, 1)
        lane = lax.broadcasted_iota(jnp.int32, prod.shape, 1)
        prod = jnp.where(lane % 2 == 0, prod, 0.0).astype(BF16)
        r = lax.broadcasted_iota(jnp.int32, (MXU_DIM, MXU_DIM // 2), 0)
        c = lax.broadcasted_iota(jnp.int32, (MXU_DIM, MXU_DIM // 2), 1)
        sel = (r == 2 * c).astype(BF16)
        for q in range(tn // MXU_DIM):
            part = jnp.dot(prod[:, q * MXU_DIM:(q + 1) * MXU_DIM], sel, preferred_element_type=F32)
            o_ref[:, q * (MXU_DIM // 2):(q + 1) * (MXU_DIM // 2)] = part.astype(o_ref.dtype)


def _moe_down_kernel(te_ref, nu_ref, h_ref, w_ref, b_ref, o_ref, wbf_ref):
    i = pl.program_id(1)

    @pl.when(_weights_changed(te_ref, i))
    def _():
        wbf_ref[...] = w_ref[0, 0].astype(BF16)

    @pl.when(i < nu_ref[0])
    def _():
        o_ref[...] = (jnp.dot(h_ref[...], wbf_ref[...], preferred_element_type=F32) + b_ref[0]).astype(o_ref.dtype)


def _moe_experts(x_rows, tile_expert, n_used, w_up, b_up, w_down, b_down, layer):
    n_rows = x_rows.shape[0]
    n_tiles = n_rows // MOE_TM
    tm = MOE_TM

    def row_tile(j, i, te, nu):
        return (jnp.minimum(i, nu[0] - 1), 0)

    tn = MOE_UP_TN
    hact = pl.pallas_call(
        _moe_up_kernel,
        grid_spec=pltpu.PrefetchScalarGridSpec(
            num_scalar_prefetch=2,
            grid=(2 * D_EXPERT // tn, n_tiles),
            in_specs=[pl.BlockSpec((tm, D_MODEL), row_tile),
                      pl.BlockSpec((1, 1, D_MODEL, tn), lambda j, i, te, nu: (layer, te[i], 0, j)),
                      pl.BlockSpec((1, 1, tn), lambda j, i, te, nu: (layer * N_EXPERTS + te[i], 0, j))],
            out_specs=pl.BlockSpec((tm, tn // 2), lambda j, i, te, nu: (jnp.minimum(i, nu[0] - 1), j)),
            scratch_shapes=[pltpu.VMEM((D_MODEL, tn), BF16)]),
        out_shape=jax.ShapeDtypeStruct((n_rows, D_EXPERT), BF16),
        compiler_params=pltpu.CompilerParams(dimension_semantics=("arbitrary", "arbitrary"),
                                             vmem_limit_bytes=VMEM_LIMIT_BYTES),
        name="moe_up_swiglu",
    )(tile_expert, n_used, x_rows, w_up, b_up.reshape(DEPTH * N_EXPERTS, 1, 2 * D_EXPERT))

    tn = MOE_DOWN_TN
    return pl.pallas_call(
        _moe_down_kernel,
        grid_spec=pltpu.PrefetchScalarGridSpec(
            num_scalar_prefetch=2,
            grid=(D_MODEL // tn, n_tiles),
            in_specs=[pl.BlockSpec((tm, D_EXPERT), row_tile),
                      pl.BlockSpec((1, 1, D_EXPERT, tn), lambda j, i, te, nu: (layer, te[i], 0, j)),
                      pl.BlockSpec((1, 1, tn), lambda j, i, te, nu: (layer * N_EXPERTS + te[i], 0, j))],
            out_specs=pl.BlockSpec((tm, tn), lambda j, i, te, nu: (jnp.minimum(i, nu[0] - 1), j)),
            scratch_shapes=[pltpu.VMEM((D_EXPERT, tn), BF16)]),
        out_shape=jax.ShapeDtypeStruct((n_rows, D_MODEL), BF16),
        compiler_params=pltpu.CompilerParams(dimension_semantics=("arbitrary", "arbitrary"),
                                             vmem_limit_bytes=VMEM_LIMIT_BYTES),
        name="moe_down",
    )(tile_expert, n_used, hact, w_down, b_down.reshape(DEPTH * N_EXPERTS, 1, D_MODEL))


DN_VH_STEP = 8
DN_QKH_STEP = DN_VH_STEP // (DN_V_HEADS // DN_QK_HEADS)
DN_GROUPS = DN_V_HEADS // DN_VH_STEP
DN_BASE_BLOCK = 16


def _dot(a, b):
    return jnp.dot(a, b, preferred_element_type=F32)


def _dot_nt(a, b):
    return lax.dot_general(a, b, (((1,), (1,)), ((), ())), preferred_element_type=F32)


def _split(a):
    hi = a.astype(BF16)
    return hi, (a - hi.astype(F32)).astype(BF16)


def _dot3(a, b, dot=_dot):
    return dot(a[0], b[0]) + (dot(a[0], b[1]) + dot(a[1], b[0]))


def _unit_lower_inverses(lows, ii, jj):
    eye = (ii == jj).astype(F32)
    blk16 = (ii // DN_BASE_BLOCK) == (jj // DN_BASE_BLOCK)
    blk32 = (ii // (2 * DN_BASE_BLOCK)) == (jj // (2 * DN_BASE_BLOCK))
    ps = [jnp.where(blk16, -low, 0.0) for low in lows]
    ts = [eye + p for p in ps]
    for _ in range(3):
        pss = [_split(p) for p in ps]
        ps = [_dot3(s, s) for s in pss]
        ts = [t + _dot3(_split(t), _split(p)) for t, p in zip(ts, ps)]
    for sel in (blk32 & ~blk16, ~blk32):
        tss = [_split(t) for t in ts]
        us = [_dot3(_split(jnp.where(sel, low, 0.0)), s) for low, s in zip(lows, tss)]
        ts = [t - _dot3(s, _split(u)) for t, s, u in zip(ts, tss, us)]
    return ts


def _dn_kernel(q_ref, k_ref, v_ref, z_ref, gate_ref, cq_ref, ck_ref, cv_ref, s0_ref, wq_ref, wk_ref, wv_ref,
               hp_ref, nw_ref, o_ref, s_ref, pq_ref, pk_ref, pv_ref, *, n_valid):
    c = DN_CHUNK
    hd = DN_HEAD_DIM

    @pl.when(pl.program_id(2) == 0)
    def _():
        pq_ref[c - SUBLANES:c, :] = cq_ref[0]
        pk_ref[c - SUBLANES:c, :] = ck_ref[0]
        pv_ref[c - SUBLANES:c, :] = cv_ref[0]
        s_ref[...] = s0_ref[...]

    row = lax.broadcasted_iota(jnp.int32, (c, 1), 0)
    ii = lax.broadcasted_iota(jnp.int32, (c, c), 0)
    jj = lax.broadcasted_iota(jnp.int32, (c, c), 1)

    def conv_silu(u_ref, prev_ref, w_ref):
        u = u_ref[...]
        prev = prev_ref[...]
        y = u * w_ref[DN_CONV_W - 1:DN_CONV_W, :]
        for s in range(1, DN_CONV_W):
            shifted = jnp.where(row >= s, pltpu.roll(u, s, 0), pltpu.roll(prev, s, 0))
            y = y + shifted * w_ref[DN_CONV_W - 1 - s:DN_CONV_W - s, :]
        prev_ref[...] = u
        return y * jax.nn.sigmoid(y)

    q_all = conv_silu(q_ref, pq_ref, wq_ref)
    k_all = conv_silu(k_ref, pk_ref, wk_ref)
    v_all = conv_silu(v_ref, pv_ref, wv_ref)

    gt = gate_ref[...]
    beta_all = jax.nn.sigmoid(gt)
    pre = gt + hp_ref[0, 0:1, :]
    softplus = jnp.maximum(pre, 0.0) + jnp.log1p(jnp.exp(-jnp.abs(pre)))
    g_all = -jnp.exp(hp_ref[0, 1:2, :]) * softplus
    if n_valid < c:
        live = row < n_valid
        beta_all = jnp.where(live, beta_all, 0.0)
        g_all = jnp.where(live, g_all, 0.0)
        k_all = jnp.where(live, k_all, 0.0)
        v_all = jnp.where(live, v_all, 0.0)
    gc = g_all
    s = 1
    while s < c:
        gc = gc + jnp.where(row >= s, pltpu.roll(gc, s, 0), 0.0)
        s *= 2
    gc_t = gc.T

    def l2n(t):
        return t * lax.rsqrt(jnp.sum(t * t, axis=-1, keepdims=True) + NORM_EPS)

    heads = range(DN_VH_STEP)
    rep = DN_VH_STEP // DN_QKH_STEP
    q_n = [l2n(q_all[:, j * hd:(j + 1) * hd]) * hd ** -0.5 for j in range(DN_QKH_STEP)]
    k_n = [l2n(k_all[:, j * hd:(j + 1) * hd]) for j in range(DN_QKH_STEP)]
    kk_qk = [_dot3(_split(jnp.concatenate([k, q], axis=0)), _split(k), _dot_nt) for k, q in zip(k_n, q_n)]
    g_col = [gc[:, SUBLANES + h:SUBLANES + h + 1] for h in heads]
    beta = [beta_all[:, h:h + 1] for h in heads]
    decay = [jnp.exp(jnp.minimum(g_col[h] - gc_t[SUBLANES + h:SUBLANES + h + 1, :], 0.0)) for h in heads]
    lows = [jnp.where(ii > jj, beta[h] * kk_qk[h // rep][:c] * decay[h], 0.0) for h in heads]
    t_inv = _unit_lower_inverses(lows, ii, jj)
    e_col = [jnp.exp(g) for g in g_col]
    rhs = [jnp.concatenate([v_all[:, h * hd:(h + 1) * hd] * beta[h], k_n[h // rep] * (beta[h] * e_col[h])], axis=1)
           for h in heads]
    w = [_dot3(_split(t_inv[h]), _split(rhs[h])) for h in heads]
    states = [s_ref[0, h] for h in heads]
    ks_qs = [_dot3(_split(jnp.concatenate([w[h][:, hd:], q_n[h // rep] * e_col[h]], axis=0)), _split(states[h]))
             for h in heads]
    v_new = [w[h][:, :hd] - ks_qs[h][:c] for h in heads]
    g_last = [g[c - 1:c, :] for g in g_col]
    k_tail = [k_n[h // rep] * jnp.exp(g_last[h] - g_col[h]) for h in heads]
    for h in heads:
        s_ref[0, h] = states[h] * jnp.exp(g_last[h]) + _dot3(_split(k_tail[h].T), _split(v_new[h]))
    attn = [jnp.where(ii >= jj, kk_qk[h // rep][c:] * decay[h], 0.0) for h in heads]
    outs = [ks_qs[h][c:] + _dot(attn[h].astype(BF16), v_new[h].astype(BF16)) for h in heads]
    for h in heads:
        o = outs[h]
        o = o * lax.rsqrt(jnp.mean(o * o, axis=-1, keepdims=True) + NORM_EPS) * nw_ref[...]
        z_h = z_ref[:, h * hd:(h + 1) * hd]
        o_ref[:, h * hd:(h + 1) * hd] = (o * (z_h * jax.nn.sigmoid(z_h))).astype(o_ref.dtype)


def _deltanet(proj, gates, conv_init, s0, w_conv, head_params, norm_w, *, batch, n_chunks, n_valid):
    nc = n_chunks
    qw = DN_QKH_STEP * DN_HEAD_DIM
    vw = DN_VH_STEP * DN_HEAD_DIM
    k_off = DN_KEY_DIM // qw
    v_off = 2 * DN_KEY_DIM // vw
    z_off = DN_CONV_DIM // vw
    chunk = lambda width, off: pl.BlockSpec((DN_CHUNK, width), lambda bi, g, ci: (bi * nc + ci, off + g))
    init = lambda width, off: pl.BlockSpec((1, SUBLANES, width), lambda bi, g, ci: (bi, 0, off + g))
    wspec = lambda width, off: pl.BlockSpec((DN_CONV_W, width), lambda bi, g, ci: (0, off + g))
    state = pl.BlockSpec((1, DN_VH_STEP, DN_HEAD_DIM, DN_HEAD_DIM), lambda bi, g, ci: (bi, g, 0, 0))
    return pl.pallas_call(
        functools.partial(_dn_kernel, n_valid=n_valid),
        grid=(batch, DN_GROUPS, nc),
        in_specs=[chunk(qw, 0), chunk(qw, k_off), chunk(vw, v_off), chunk(vw, z_off), chunk(LANES, 0),
                  init(qw, 0), init(qw, k_off), init(vw, v_off), state,
                  wspec(qw, 0), wspec(qw, k_off), wspec(vw, v_off),
                  pl.BlockSpec((1, SUBLANES, LANES), lambda bi, g, ci: (g, 0, 0)),
                  pl.BlockSpec((1, DN_HEAD_DIM), lambda bi, g, ci: (0, 0))],
        out_specs=[chunk(vw, 0), state],
        out_shape=[jax.ShapeDtypeStruct((batch * nc * DN_CHUNK, DN_VAL_DIM), BF16),
                   jax.ShapeDtypeStruct((batch, DN_V_HEADS, DN_HEAD_DIM, DN_HEAD_DIM), F32)],
        scratch_shapes=[pltpu.VMEM((DN_CHUNK, qw), F32), pltpu.VMEM((DN_CHUNK, qw), F32),
                        pltpu.VMEM((DN_CHUNK, vw), F32)],
        compiler_params=pltpu.CompilerParams(dimension_semantics=("arbitrary", "arbitrary", "arbitrary"),
                                             vmem_limit_bytes=VMEM_LIMIT_BYTES),
        name="deltanet_chunk",
    )(proj, proj, proj, proj, gates, conv_init, conv_init, conv_init, s0, w_conv, w_conv, w_conv,
      head_params, norm_w.reshape(1, DN_HEAD_DIM))


def _dn_gate_layout(tail):
    lead = tail.shape[:-1]
    t = tail.reshape(lead + (2, DN_GROUPS, DN_VH_STEP))
    t = jnp.swapaxes(t, -3, -2).reshape(lead + (DN_GROUPS, 2 * DN_VH_STEP))
    t = jnp.pad(t, [(0, 0)] * (len(lead) + 1) + [(0, LANES - 2 * DN_VH_STEP)])
    return t.reshape(lead + (DN_GROUPS * LANES,))


def _dn_head_params(a_log, dt_bias):
    def lay(p):
        p = p.reshape(DN_GROUPS, 1, DN_VH_STEP)
        return jnp.pad(p, ((0, 0), (0, 0), (DN_VH_STEP, LANES - 2 * DN_VH_STEP)))
    rows = jnp.concatenate([lay(dt_bias), lay(a_log)], axis=1)
    return jnp.pad(rows, ((0, 0), (0, SUBLANES - 2), (0, 0)))


ATT_BLK = 128
ATT_MASKED = -1e30


ATT_P_HEADS = (16, 8, 4)
ATT_CLASS_UNROLL = 16


def _att_prompt_kernel(slope_ref, q_ref, *refs, group, dilation, heads, has_prev):
    n_in = 4 if has_prev else 2
    kv_refs, (o_ref, lse_ref), stage = refs[:n_in], refs[n_in:n_in + 2], refs[n_in + 2:]
    hd = ATT_HEAD_DIM
    blk = ATT_BLK
    n_keys = 2 * blk if has_prev else blk
    qi = lax.broadcasted_iota(jnp.int32, (blk, n_keys), 0)
    kj = lax.broadcasted_iota(jnp.int32, (blk, n_keys), 1)
    dist = qi + (n_keys - blk) - kj
    valid = (dist >= 0) & (dist <= blk)
    if has_prev:
        valid = valid & ((kj >= blk) | (pl.program_id(1) > 0))
    dist_f = (dist * dilation).astype(F32)
    lane = lax.broadcasted_iota(jnp.int32, (blk, LANES), 1)
    head0 = pl.program_id(2) * heads

    @pl.when(pl.program_id(2) == 0)
    def _():
        lse_ref[...] = jnp.zeros(lse_ref.shape, F32)

    for h in range(heads):
        hs = slice(h * hd, (h + 1) * hd)
        slope = slope_ref[group, head0 + h]
        if dilation > 1:
            q_st, o_st = stage[0], stage[1]
            q_st[...] = q_ref[:, hs]
            for src, dst in zip(kv_refs, stage[2:]):
                dst[...] = src[:, hs]
            kv_src = stage[2:]
        else:
            kv_src = kv_refs

        def one_class(r, carry):
            if dilation > 1:
                rows = pl.ds(r, blk, stride=dilation)
                q = q_st[rows, :]
                parts = [ref[rows, :] for ref in kv_src]
            else:
                rows = pl.ds(0, blk)
                q = q_ref[:, hs]
                parts = [ref[:, hs] for ref in kv_src]
            if has_prev:
                k = jnp.concatenate([parts[1], parts[0]], axis=0)
                v = jnp.concatenate([parts[3], parts[2]], axis=0)
            else:
                k, v = parts
            sc = _dot_nt(q.astype(BF16), k.astype(BF16)) * hd ** -0.5 - slope * dist_f
            sc = jnp.where(valid, sc, ATT_MASKED)
            m = jnp.max(sc, axis=-1, keepdims=True)
            p = jnp.exp(sc - m)
            l = jnp.sum(p, axis=-1, keepdims=True)
            o = _dot(p.astype(BF16), v.astype(BF16)) / l
            if dilation > 1:
                o_st[rows, :] = o
            else:
                o_ref[:, hs] = o
            lse_ref[rows, :] = jnp.where(lane == head0 + h, m + jnp.log(l), lse_ref[rows, :])
            return carry

        if dilation > 1:
            lax.fori_loop(0, dilation, one_class, 0, unroll=ATT_CLASS_UNROLL)
            o_ref[:, hs] = o_st[...]
        else:
            one_class(0, 0)


def _att_prompt(q, kv, slopes, *, group, batch, seq):
    d = GROUP_DILATIONS[group]
    heads = ATT_P_HEADS[group]
    span = d * ATT_BLK
    assert GROUP_WINDOWS[group] // d == ATT_BLK and seq % span == 0 and HEADS_PER_GROUP % heads == 0
    ns = seq // span
    has_prev = ns > 1
    w = heads * ATT_HEAD_DIM
    n_hc = ATT_SLOT_DIM // w
    n_kv = kv.shape[1] // w
    cur = lambda col: pl.BlockSpec((span, w), lambda b, n, hc: (b * ns + n, col + hc))
    prev = lambda col: pl.BlockSpec((span, w), lambda b, n, hc: (b * ns + jnp.maximum(n - 1, 0), col + hc))
    k_col, v_col = group * n_hc, n_kv // 2 + group * n_hc
    in_specs = [pl.BlockSpec(memory_space=pltpu.SMEM), cur(group * n_hc)]
    in_specs += [cur(k_col), prev(k_col), cur(v_col), prev(v_col)] if has_prev else [cur(k_col), cur(v_col)]
    return pl.pallas_call(
        functools.partial(_att_prompt_kernel, group=group, dilation=d, heads=heads, has_prev=has_prev),
        grid=(batch, ns, n_hc),
        in_specs=in_specs,
        out_specs=[cur(0), pl.BlockSpec((span, LANES), lambda b, n, hc: (b * ns + n, 0))],
        out_shape=[jax.ShapeDtypeStruct((batch * seq, ATT_SLOT_DIM), F32),
                   jax.ShapeDtypeStruct((batch * seq, LANES), F32)],
        scratch_shapes=[pltpu.VMEM((span, ATT_HEAD_DIM), F32)] * ((4 + 2 * has_prev) if d > 1 else 0),
        compiler_params=pltpu.CompilerParams(dimension_semantics=("arbitrary", "arbitrary", "arbitrary"),
                                             vmem_limit_bytes=VMEM_LIMIT_BYTES),
        name="dilated_attention_prompt",
    )(slopes, q, *([kv] * (4 if has_prev else 2)))


ATT_S_HEADS = 4


def _att_sample_kernel(slope_ref, q_ref, kc_ref, vc_ref, kn_ref, vn_ref, o_ref, lse_ref, *, group, dilation):
    hd = ATT_HEAD_DIM
    t_new = q_ref.shape[1]
    l_buf = kc_ref.shape[1]
    window = GROUP_WINDOWS[group]
    tq = lax.broadcasted_iota(jnp.int32, (t_new, l_buf), 0)
    ic = lax.broadcasted_iota(jnp.int32, (t_new, l_buf), 1)
    dist_c = l_buf + tq - ic
    ok_c = (((ic - tq) & (dilation - 1)) == 0) & (dist_c <= window)
    tn = lax.broadcasted_iota(jnp.int32, (t_new, t_new), 0)
    un = lax.broadcasted_iota(jnp.int32, (t_new, t_new), 1)
    dist_n = tn - un
    ok_n = (dist_n >= 0) & ((dist_n & (dilation - 1)) == 0) & (dist_n <= window)
    lane = lax.broadcasted_iota(jnp.int32, (t_new, LANES), 1)
    lse_all = jnp.zeros((t_new, LANES), F32)
    for h in range(ATT_S_HEADS):
        hs = slice(h * hd, (h + 1) * hd)
        slope = slope_ref[group, pl.program_id(1) * ATT_S_HEADS + h]
        q = q_ref[0, :, hs].astype(BF16)
        sc_c = _dot_nt(q, kc_ref[0, :, hs].astype(BF16)) * hd ** -0.5 - slope * dist_c.astype(F32)
        sc_n = _dot_nt(q, kn_ref[0, :, hs].astype(BF16)) * hd ** -0.5 - slope * dist_n.astype(F32)
        sc_c = jnp.where(ok_c, sc_c, ATT_MASKED)
        sc_n = jnp.where(ok_n, sc_n, ATT_MASKED)
        m = jnp.maximum(jnp.max(sc_c, axis=-1, keepdims=True), jnp.max(sc_n, axis=-1, keepdims=True))
        p_c = jnp.exp(sc_c - m)
        p_n = jnp.exp(sc_n - m)
        l = jnp.sum(p_c, axis=-1, keepdims=True) + jnp.sum(p_n, axis=-1, keepdims=True)
        acc = (_dot(p_c.astype(BF16), vc_ref[0, :, hs].astype(BF16))
               + _dot(p_n.astype(BF16), vn_ref[0, :, hs].astype(BF16)))
        o_ref[0, :, hs] = acc / l
        lse_all = jnp.where(lane == h, m + jnp.log(l), lse_all)
    lse_ref[0, 0] = lse_all


def _att_sample(q, kv_new, cache, slopes, *, group):
    b, t, _ = q.shape
    l_buf = cache.shape[1]
    w = ATT_S_HEADS * ATT_HEAD_DIM
    per_group = ATT_SLOT_DIM // w
    n_q = q.shape[2] // w
    cache2 = cache.reshape(b, l_buf, 2 * ATT_SLOT_DIM)
    new = lambda col: pl.BlockSpec((1, t, w), lambda bi, hc: (bi, 0, col + hc))
    buf = lambda col: pl.BlockSpec((1, l_buf, w), lambda bi, hc: (bi, 0, col + hc))
    o, lse = pl.pallas_call(
        functools.partial(_att_sample_kernel, group=group, dilation=GROUP_DILATIONS[group]),
        grid=(b, per_group),
        in_specs=[pl.BlockSpec(memory_space=pltpu.SMEM), new(group * per_group), buf(0), buf(per_group),
                  new(group * per_group), new(n_q + group * per_group)],
        out_specs=[new(0), pl.BlockSpec((1, 1, t, LANES), lambda bi, hc: (bi, hc, 0, 0))],
        out_shape=[jax.ShapeDtypeStruct((b, t, ATT_SLOT_DIM), F32),
                   jax.ShapeDtypeStruct((b, per_group, t, LANES), F32)],
        compiler_params=pltpu.CompilerParams(dimension_semantics=("arbitrary", "arbitrary"),
                                             vmem_limit_bytes=VMEM_LIMIT_BYTES),
        name="dilated_attention_sample",
    )(slopes, q, cache2, cache2, kv_new, kv_new)
    lse = lse[..., :ATT_S_HEADS].transpose(0, 2, 1, 3).reshape(b * t, HEADS_PER_GROUP)
    return o.reshape(b * t, ATT_SLOT_DIM), jnp.pad(lse, ((0, 0), (0, LANES - HEADS_PER_GROUP)))


def _att_merge_kernel(o0_ref, o1_ref, o2_ref, l0_ref, l1_ref, l2_ref, out_ref):
    hd = ATT_HEAD_DIM
    lses = [l0_ref[...], l1_ref[...], l2_ref[...]]
    m = jnp.maximum(jnp.maximum(lses[0], lses[1]), lses[2])
    zs = [jnp.exp(l - m) for l in lses]
    inv = 1.0 / (zs[0] + zs[1] + zs[2])
    for h in range(HEADS_PER_GROUP):
        hs = slice(h * hd, (h + 1) * hd)
        acc = (zs[0][:, h:h + 1] * inv[:, h:h + 1]) * o0_ref[:, hs]
        acc = acc + (zs[1][:, h:h + 1] * inv[:, h:h + 1]) * o1_ref[:, hs]
        acc = acc + (zs[2][:, h:h + 1] * inv[:, h:h + 1]) * o2_ref[:, hs]
        out_ref[:, hs] = acc.astype(out_ref.dtype)


def _att_merge(outs, lses):
    n = outs[0].shape[0]
    tm = 256 if n % 256 == 0 else n
    ospec = pl.BlockSpec((tm, ATT_SLOT_DIM), lambda i: (i, 0))
    lspec = pl.BlockSpec((tm, LANES), lambda i: (i, 0))
    return pl.pallas_call(
        _att_merge_kernel,
        grid=(n // tm,),
        in_specs=[ospec] * 3 + [lspec] * 3,
        out_specs=ospec,
        out_shape=jax.ShapeDtypeStruct((n, ATT_SLOT_DIM), BF16),
        compiler_params=pltpu.CompilerParams(dimension_semantics=("arbitrary",), vmem_limit_bytes=VMEM_LIMIT_BYTES),
        name="attention_group_merge",
    )(*outs, *lses)


NORM_TM = 192


def _postnorm_kernel(x_ref, *rest, n_terms, with_router):
    if n_terms:
        gates = rest[n_terms][...]
        f = gates[:, 0:1] * rest[0][...].astype(F32)
        for k in range(1, n_terms):
            f = f + gates[:, k:k + 1] * rest[k][...].astype(F32)
        rest = rest[n_terms + 1:]
    else:
        f = rest[0][...]
        rest = rest[1:]
    g_ref, b_ref = rest[:2]
    rest = rest[2:]
    y = DEEP_ALPHA * x_ref[...] + f
    mu = jnp.mean(y, axis=-1, keepdims=True)
    yc = y - mu
    var = jnp.mean(yc * yc, axis=-1, keepdims=True)
    out = yc * lax.rsqrt(var + LN_EPS) * g_ref[...] + b_ref[...]
    if with_router:
        wr_ref, br_ref, o_ref, obf_ref, lg_ref = rest
        lg_ref[...] = jnp.dot(out, wr_ref[...], precision=lax.Precision.HIGHEST,
                              preferred_element_type=F32) + br_ref[...]
    else:
        o_ref, obf_ref = rest
    o_ref[...] = out
    obf_ref[...] = out.astype(BF16)


def _postnorm(x, f, g, b, w_router=None, b_router=None, gates=None):
    n, d = x.shape
    tm = NORM_TM if n % NORM_TM == 0 else n
    row = pl.BlockSpec((tm, d), lambda i: (i, 0))
    vec = pl.BlockSpec((1, d), lambda i: (0, 0))
    with_router = w_router is not None
    n_terms = 0 if gates is None else gates.shape[1]
    if n_terms:
        terms = [pl.BlockSpec((tm, d), lambda i, k=k: (k * (n // tm) + i, 0)) for k in range(n_terms)]
        in_specs = [row] + terms + [pl.BlockSpec((tm, n_terms), lambda i: (i, 0)), vec, vec]
        args = [x] + [f] * n_terms + [gates, g.reshape(1, d), b.reshape(1, d)]
    else:
        in_specs = [row, row, vec, vec]
        args = [x, f, g.reshape(1, d), b.reshape(1, d)]
    out_specs = [row, row]
    out_shape = [jax.ShapeDtypeStruct((n, d), F32), jax.ShapeDtypeStruct((n, d), BF16)]
    if with_router:
        in_specs += [pl.BlockSpec((d, N_EXPERTS), lambda i: (0, 0)), pl.BlockSpec((1, N_EXPERTS), lambda i: (0, 0))]
        args += [w_router, b_router.reshape(1, N_EXPERTS)]
        out_specs.append(pl.BlockSpec((tm, N_EXPERTS), lambda i: (i, 0)))
        out_shape.append(jax.ShapeDtypeStruct((n, N_EXPERTS), F32))
    return pl.pallas_call(
        functools.partial(_postnorm_kernel, n_terms=n_terms, with_router=with_router),
        grid=(n // tm,),
        in_specs=in_specs, out_specs=out_specs, out_shape=out_shape,
        compiler_params=pltpu.CompilerParams(dimension_semantics=("arbitrary",), vmem_limit_bytes=VMEM_LIMIT_BYTES),
        name="postnorm_router" if with_router else "postnorm",
    )(*args)


def _alibi_slopes():
    h = jnp.arange(1, N_ATT_HEADS + 1, dtype=F32)
    return (2.0 ** (-ALIBI_MAX_BIAS * h / N_ATT_HEADS)).reshape(N_GROUPS, HEADS_PER_GROUP)


def _moe_block(x_bf, logits, w_up, b_up, w_down, b_down, layer):
    row_of, src_tok, gates, tile_expert, n_used = _route(logits)
    x_rows = jnp.take(x_bf, src_tok, axis=0, mode="clip")
    y_rows = _moe_experts(x_rows, tile_expert, n_used, w_up, b_up, w_down, b_down, layer)
    return jnp.take(y_rows, row_of.T.reshape(-1), axis=0, mode="clip"), gates


def kernel(x_prompt, x_sample, state_dn_S, state_dn_conv, cache_kv_w128, cache_kv_w512, cache_kv_w2048,
           w_dn_in, w_dn_conv, dn_a_log, dn_dt_bias, dn_norm_w, w_dn_out, w_kv_shared, w_att_q, w_att_out,
           ln_g, ln_b, w_router, b_router, w_up, b_up, w_down, b_down):
    bp, sp, _ = x_prompt.shape
    bs, ts, _ = x_sample.shape
    n_p, n_s = bp * sp, bs * ts
    kv_caches = (cache_kv_w128, cache_kv_w512, cache_kv_w2048)
    x = jnp.concatenate([x_prompt.reshape(n_p, D_MODEL), x_sample.reshape(n_s, D_MODEL)], axis=0)

    w_in = w_dn_in.reshape(D_MODEL, DN_IN_DIM)
    x_bf = x.astype(BF16)
    n_main = DN_CONV_DIM + DN_VAL_DIM
    proj = _dense(x_bf, w_in, n_cols=n_main)
    tail = _dense(x_bf, w_in, col_block_offset=n_main // LANES, n_cols=LANES, tn=LANES)[:, :2 * DN_V_HEADS]
    gates = _dn_gate_layout(tail)
    head_params = _dn_head_params(dn_a_log[0], dn_dt_bias[0])
    pad_s = DN_CHUNK - ts
    proj_s = proj[n_p:].reshape(bs, ts, n_main)
    o_p, s_p = _deltanet(proj, gates, jnp.zeros((bp, SUBLANES, DN_CONV_DIM), F32),
                         jnp.zeros((bp, DN_V_HEADS, DN_HEAD_DIM, DN_HEAD_DIM), F32), w_dn_conv[0], head_params,
                         dn_norm_w[0], batch=bp, n_chunks=sp // DN_CHUNK, n_valid=DN_CHUNK)
    o_s, s_s = _deltanet(jnp.pad(proj_s, ((0, 0), (0, pad_s), (0, 0))).reshape(bs * DN_CHUNK, n_main),
                         jnp.pad(gates[n_p:].reshape(bs, ts, -1), ((0, 0), (0, pad_s), (0, 0))).reshape(bs * DN_CHUNK, -1),
                         jnp.pad(state_dn_conv[0], ((0, 0), (SUBLANES - (DN_CONV_W - 1), 0), (0, 0))),
                         state_dn_S[0], w_dn_conv[0], head_params, dn_norm_w[0], batch=bs, n_chunks=1, n_valid=ts)
    cbuf_p = jnp.stack([proj[(b + 1) * sp - (DN_CONV_W - 1):(b + 1) * sp, :DN_CONV_DIM] for b in range(bp)])
    cbuf_s = proj_s[:, ts - (DN_CONV_W - 1):, :DN_CONV_DIM]
    o = jnp.concatenate([o_p, o_s.reshape(bs, DN_CHUNK, DN_VAL_DIM)[:, :ts].reshape(n_s, DN_VAL_DIM)], axis=0)
    mix = _dense(o, w_dn_out.reshape(DN_VAL_DIM, D_MODEL))
    x, x_bf, logits = _postnorm(x, mix, ln_g[0, 0], ln_b[0, 0], w_router[0], b_router[0])
    ffn, gates = _moe_block(x_bf, logits, w_up, b_up, w_down, b_down, 0)
    x, x_bf = _postnorm(x, ffn, ln_g[0, 1], ln_b[0, 1], gates=gates)

    kv = _dense(x_bf, w_kv_shared)
    def window_rows(first_row, n_seq, seq_len, g, length):
        def part(b, sel):
            r1 = first_row + (b + 1) * seq_len
            c0 = (sel * N_GROUPS + g) * ATT_SLOT_DIM
            return kv[r1 - length:r1, c0:c0 + ATT_SLOT_DIM].reshape(length, 1, HEADS_PER_GROUP, ATT_HEAD_DIM)
        return jnp.stack([jnp.concatenate([part(b, 0), part(b, 1)], axis=1) for b in range(n_seq)])

    new_kv_p = [window_rows(0, bp, sp, g, min(GROUP_WINDOWS[g], sp)) for g in range(N_GROUPS)]
    new_kv_s = [jnp.concatenate([kv_caches[g][:, ts:], window_rows(n_p, bs, ts, g, ts)], axis=1)
                for g in range(N_GROUPS)]

    slopes = _alibi_slopes()
    q = _dense(x_bf, w_att_q.reshape(D_MODEL, N_ATT_HEADS * ATT_HEAD_DIM))
    q_s = q[n_p:].reshape(bs, ts, -1)
    kv_s_rows = kv[n_p:].reshape(bs, ts, -1)
    outs_p, lses_p, outs_s, lses_s = [], [], [], []
    for g in range(N_GROUPS):
        o_g, lse_g = _att_prompt(q, kv, slopes, group=g, batch=bp, seq=sp)
        outs_p.append(o_g)
        lses_p.append(lse_g)
        o_g, lse_g = _att_sample(q_s, kv_s_rows, kv_caches[g], slopes, group=g)
        outs_s.append(o_g)
        lses_s.append(lse_g)
    att = jnp.concatenate([_att_merge(outs_p, lses_p), _att_merge(outs_s, lses_s)], axis=0)
    mix = _dense(att, w_att_out.reshape(ATT_SLOT_DIM, D_MODEL))
    x, x_bf, logits = _postnorm(x, mix, ln_g[1, 0], ln_b[1, 0], w_router[1], b_router[1])
    ffn, gates = _moe_block(x_bf, logits, w_up, b_up, w_down, b_down, 1)
    x, _ = _postnorm(x, ffn, ln_g[1, 1], ln_b[1, 1], gates=gates)

    y_prompt = x[:n_p].reshape(bp, sp, D_MODEL)
    y_sample = x[n_p:].reshape(bs, ts, D_MODEL)
    return (y_prompt, y_sample, s_p[None], cbuf_p[None], new_kv_p[0], new_kv_p[1], new_kv_p[2],
            s_s[None], cbuf_s[None], new_kv_s[0], new_kv_s[1], new_kv_s[2])
```
